```python
import math
import jax, jax.numpy as jnp
from jax import lax
import numpy as np

D_MODEL = 1024
BATCH = 8
SEQ = 4096
DEPTH = 2

PLE_DIM = 256
D_FF = 2816
N_NORMS = 4
EPS = 1e-6
NEG_INF = -1e30

ATT_HEADS = 4
ATT_HEAD_DIM = 64
ATT_WIDTH = ATT_HEADS * ATT_HEAD_DIM
Q_BLOCK = 128

POOL_WINDOWS = (2, 4, 8, 16)
POOL_GROUPS = len(POOL_WINDOWS)
POOL_GROUP_DIM = 64
POOL_WIDTH = POOL_GROUPS * POOL_GROUP_DIM

SSM_GROUP_DIM = 16
SSM_GROUPS = 16
SSM_WIDTH = SSM_GROUPS * SSM_GROUP_DIM
SSM_STATE = 64
SSM_DT_MIN = 0.001
SSM_DT_MAX = 0.1

CONV_CHANNELS = 256
CONV_WIDTH = 3

N_BRANCH = 4
BRANCH_WIDTH = 256

IN_SPLITS = (ATT_WIDTH, ATT_WIDTH, ATT_WIDTH, ATT_HEADS, POOL_WIDTH, SSM_WIDTH,
             CONV_CHANNELS, CONV_CHANNELS, CONV_CHANNELS, N_BRANCH * D_MODEL)
IN_COLS = sum(IN_SPLITS)

kernel_name = 'hybrid_gated_mixer_block'


def rmsnorm(x, g):
    xf = x.astype(jnp.float32)
    y = xf * lax.rsqrt(jnp.mean(xf * xf, axis=-1, keepdims=True) + EPS)
    return (y * g.astype(jnp.float32)).astype(x.dtype)


def swiglu(x, w_gate, w_up, w_down):
    return (jax.nn.silu(x @ w_gate) * (x @ w_up)) @ w_down


def forgetting_attention(q, k, v, f_logit, f_bias):
    B, S, H, Dh = q.shape
    nb = S // Q_BLOCK
    log_f = jax.nn.log_sigmoid(f_logit.astype(jnp.float32) + f_bias.astype(jnp.float32))
    c = jnp.cumsum(log_f, axis=1)
    qf = q.astype(jnp.float32) * (Dh ** -0.5)
    kf = k.astype(jnp.float32)
    vf = v.astype(jnp.float32)
    q_blk = qf.reshape(B, nb, Q_BLOCK, H, Dh).transpose(1, 0, 3, 2, 4)
    c_blk = c.reshape(B, nb, Q_BLOCK, H).transpose(1, 0, 3, 2)
    pos_blk = jnp.arange(S, dtype=jnp.int32).reshape(nb, Q_BLOCK)
    c_k = c.transpose(0, 2, 1)
    k_pos = jnp.arange(S, dtype=jnp.int32)

    def one_block(args):
        qb, cb, pb = args
        s = jnp.einsum('bhqd,bkhd->bhqk', qb, kf)
        s = s + (cb[..., :, None] - c_k[:, :, None, :])
        mask = k_pos[None, :] <= pb[:, None]
        s = jnp.where(mask[None, None], s, NEG_INF)
        pr = jax.nn.softmax(s, axis=-1)
        return jnp.einsum('bhqk,bkhd->bqhd', pr, vf)

    out = lax.map(one_block, (q_blk, c_blk, pos_blk))
    out = out.transpose(1, 0, 2, 3, 4).reshape(B, S, H * Dh)
    return out.astype(q.dtype)


def multiscale_pool(xp, pool_w, pool_scale):
    B, S, C = xp.shape
    xf = xp.astype(jnp.float32)
    csp = jnp.concatenate([jnp.zeros((B, 1, C), jnp.float32), jnp.cumsum(xf, axis=1)], axis=1)
    win = jnp.repeat(jnp.array(POOL_WINDOWS, jnp.int32), POOL_GROUP_DIM)
    t = jnp.arange(S, dtype=jnp.int32)[:, None]
    lo = jnp.maximum(t + 1 - win[None, :], 0)
    ch = jnp.arange(C, dtype=jnp.int32)[None, :]
    window_sum = csp[:, 1:, :] - csp[:, lo, ch]
    count = jnp.minimum(t + 1, win[None, :]).astype(jnp.float32)
    pooled = window_sum / count - xf
    grp = pooled.reshape(B, S, POOL_GROUPS, POOL_GROUP_DIM)
    y = jnp.einsum('bsgc,gcd->bsgd', grp, pool_w.astype(jnp.float32)).reshape(B, S, C)
    return (y * pool_scale.astype(jnp.float32)).astype(xp.dtype)


def _ssm_combine(left, right):
    a1, b1 = left
    a2, b2 = right
    return a1 * a2, a2 * b1 + b2


def s5_ssm(u, lam_re, lam_im, log_dt, b_re, b_im, c_re, c_im, d_skip, w_glu):
    B, S, _ = u.shape
    uf = u.astype(jnp.float32).reshape(B, S, SSM_GROUPS, SSM_GROUP_DIM)
    dt = jnp.exp(log_dt.astype(jnp.float32))[:, None]
    lam = lax.complex(jnp.minimum(lam_re.astype(jnp.float32), -1e-4), lam_im.astype(jnp.float32))
    lam_bar = jnp.exp(lam * dt)
    bmat = lax.complex(b_re.astype(jnp.float32), b_im.astype(jnp.float32))
    b_bar = ((lam_bar - 1.0) / lam)[..., None] * bmat
    bu = jnp.einsum('bsgh,gph->bsgp', uf.astype(jnp.complex64), b_bar)
    a = jnp.broadcast_to(lam_bar, bu.shape)
    _, states = lax.associative_scan(_ssm_combine, (a, bu), axis=1)
    cmat = lax.complex(c_re.astype(jnp.float32), c_im.astype(jnp.float32))
    y = jnp.real(jnp.einsum('bsgp,ghp->bsgh', states, cmat))
    y = y + d_skip.astype(jnp.float32).reshape(SSM_GROUPS, SSM_GROUP_DIM) * uf
    y = y.reshape(B, S, SSM_WIDTH)
    val, gate = jnp.split(y @ w_glu.astype(jnp.float32), 2, axis=-1)
    return (val * jax.nn.sigmoid(gate)).astype(u.dtype)


def short_conv(b_gate, c_gate, xin, conv_w):
    S = xin.shape[1]
    z = c_gate * xin
    zp = jnp.pad(z, ((0, 0), (CONV_WIDTH - 1, 0), (0, 0)))
    y = conv_w[0] * zp[:, 0:S]
    for j in range(1, CONV_WIDTH):
        y = y + conv_w[j] * zp[:, j:j + S]
    return b_gate * y


def hybrid_mixer(u, w_in, f_bias, pool_w, pool_scale, lam_re, lam_im, log_dt, b_re, b_im,
                 c_re, c_im, d_skip, w_glu, conv_w, w_branch, w_out):
    B, S, _ = u.shape
    z = u @ w_in
    q, k, v, f, xp, xs, cb, cc, cx, g = jnp.split(z, np.cumsum(IN_SPLITS)[:-1].tolist(), axis=-1)
    shp = (B, S, ATT_HEADS, ATT_HEAD_DIM)
    y_att = forgetting_attention(q.reshape(shp), k.reshape(shp), v.reshape(shp), f, f_bias)
    y_pool = multiscale_pool(xp, pool_w, pool_scale)
    y_ssm = s5_ssm(xs, lam_re, lam_im, log_dt, b_re, b_im, c_re, c_im, d_skip, w_glu)
    y_conv = short_conv(cb, cc, cx, conv_w)
    ys = jnp.stack([y_att.astype(u.dtype), y_pool.astype(u.dtype),
                    y_ssm.astype(u.dtype), y_conv.astype(u.dtype)], axis=2)
    proj = jnp.einsum('bsnc,ncd->bsnd', ys, w_branch)
    gates = jax.nn.sigmoid(g.reshape(B, S, N_BRANCH, D_MODEL))
    merged = jnp.sum(gates * proj, axis=2)
    return merged @ w_out


def _normal(k, shape, std):
    return std * jax.random.normal(k, shape, jnp.float32)


def setup_inputs(seed: int = 0) -> dict:
    key = jax.random.key(seed)
    ks = jax.random.split(key, 24)
    G, P, H = SSM_GROUPS, SSM_STATE, SSM_GROUP_DIM
    x = _normal(ks[0], (BATCH, SEQ, D_MODEL), 1.0)
    p = _normal(ks[1], (DEPTH, BATCH, SEQ, PLE_DIM), 1.0)
    norm_g = 1.0 + _normal(ks[2], (DEPTH, N_NORMS, D_MODEL), 0.05)
    ffn_w_gate = _normal(ks[3], (DEPTH, 2, D_MODEL, D_FF), D_MODEL ** -0.5)
    ffn_w_up = _normal(ks[4], (DEPTH, 2, D_MODEL, D_FF), D_MODEL ** -0.5)
    ffn_w_down = _normal(ks[5], (DEPTH, 2, D_FF, D_MODEL), D_FF ** -0.5)
    w_in = _normal(ks[6], (DEPTH, D_MODEL, IN_COLS), D_MODEL ** -0.5)
    f_bias = 1.0 + 4.0 * jax.random.uniform(ks[7], (DEPTH, ATT_HEADS), jnp.float32)
    pool_w = _normal(ks[8], (DEPTH, POOL_GROUPS, POOL_GROUP_DIM, POOL_GROUP_DIM), POOL_GROUP_DIM ** -0.5)
    pool_scale = 1.0 + _normal(ks[9], (DEPTH, POOL_WIDTH), 0.1)
    ssm_lam_re = -0.5 * (1.0 + _normal(ks[10], (DEPTH, G, P), 0.01))
    ssm_lam_im = jnp.tile(math.pi * jnp.arange(P, dtype=jnp.float32), (DEPTH, G, 1))
    ssm_log_dt = math.log(SSM_DT_MIN) + jax.random.uniform(ks[11], (DEPTH, G), jnp.float32) * (
        math.log(SSM_DT_MAX) - math.log(SSM_DT_MIN))
    ssm_b_re = _normal(ks[12], (DEPTH, G, P, H), (2 * H) ** -0.5)
    ssm_b_im = _normal(ks[13], (DEPTH, G, P, H), (2 * H) ** -0.5)
    ssm_c_re = _normal(ks[14], (DEPTH, G, H, P), P ** -0.5)
    ssm_c_im = _normal(ks[15], (DEPTH, G, H, P), P ** -0.5)
    ssm_d = _normal(ks[16], (DEPTH, SSM_WIDTH), 1.0)
    ssm_w_glu = _normal(ks[17], (DEPTH, SSM_WIDTH, 2 * SSM_WIDTH), SSM_WIDTH ** -0.5)
    conv_w = _normal(ks[18], (DEPTH, CONV_WIDTH, CONV_CHANNELS), CONV_WIDTH ** -0.5)
    w_branch = _normal(ks[19], (DEPTH, N_BRANCH, BRANCH_WIDTH, D_MODEL), BRANCH_WIDTH ** -0.5)
    w_out = _normal(ks[20], (DEPTH, D_MODEL, D_MODEL), D_MODEL ** -0.5)
    ple_w_gate = _normal(ks[21], (DEPTH, D_MODEL, D_MODEL), D_MODEL ** -0.5)
    ple_w_proj = _normal(ks[22], (DEPTH, PLE_DIM, D_MODEL), PLE_DIM ** -0.5)
    final_g = 1.0 + _normal(ks[23], (D_MODEL,), 0.05)
    return {'x': x, 'p': p, 'norm_g': norm_g, 'ffn_w_gate': ffn_w_gate, 'ffn_w_up': ffn_w_up,
            'ffn_w_down': ffn_w_down, 'w_in': w_in, 'f_bias': f_bias, 'pool_w': pool_w,
            'pool_scale': pool_scale, 'ssm_lam_re': ssm_lam_re, 'ssm_lam_im': ssm_lam_im,
            'ssm_log_dt': ssm_log_dt, 'ssm_b_re': ssm_b_re, 'ssm_b_im': ssm_b_im,
            'ssm_c_re': ssm_c_re, 'ssm_c_im': ssm_c_im, 'ssm_d': ssm_d, 'ssm_w_glu': ssm_w_glu,
            'conv_w': conv_w, 'w_branch': w_branch, 'w_out': w_out, 'ple_w_gate': ple_w_gate,
            'ple_w_proj': ple_w_proj, 'final_g': final_g}


def reference(x, p, norm_g, ffn_w_gate, ffn_w_up, ffn_w_down, w_in, f_bias, pool_w, pool_scale,
              ssm_lam_re, ssm_lam_im, ssm_log_dt, ssm_b_re, ssm_b_im, ssm_c_re, ssm_c_im, ssm_d,
              ssm_w_glu, conv_w, w_branch, w_out, ple_w_gate, ple_w_proj, final_g):
    h = x
    for i in range(DEPTH):
        h = h + 0.5 * swiglu(rmsnorm(h, norm_g[i, 0]), ffn_w_gate[i, 0], ffn_w_up[i, 0], ffn_w_down[i, 0])
        h = h + hybrid_mixer(rmsnorm(h, norm_g[i, 1]), w_in[i], f_bias[i], pool_w[i], pool_scale[i],
                             ssm_lam_re[i], ssm_lam_im[i], ssm_log_dt[i], ssm_b_re[i], ssm_b_im[i],
                             ssm_c_re[i], ssm_c_im[i], ssm_d[i], ssm_w_glu[i], conv_w[i],
                             w_branch[i], w_out[i])
        h = h + 0.5 * swiglu(rmsnorm(h, norm_g[i, 2]), ffn_w_gate[i, 1], ffn_w_up[i, 1], ffn_w_down[i, 1])
        gate = jax.nn.sigmoid(rmsnorm(h, norm_g[i, 3]) @ ple_w_gate[i])
        h = h + gate * (p[i] @ ple_w_proj[i])
    return rmsnorm(h, final_g)
```

```python
import functools

import jax
import jax.numpy as jnp
from jax import lax
from jax.experimental import pallas as pl
from jax.experimental.pallas import tpu as pltpu

F32 = jnp.float32
BF16 = jnp.bfloat16

EPS = 1e-6
NEG_INF = -1e30

ATT_HEADS = 4
ATT_HEAD_DIM = 64
BRANCH_WIDTH = 256
N_BRANCH = 4
POOL_WINDOWS = (2, 4, 8, 16)
POOL_GROUP_DIM = 64
POOL_HALO = 16
SSM_GROUPS = 16
SSM_GROUP_DIM = 16
SSM_STATE = 64
SSM_CHANNELS = SSM_GROUPS * SSM_STATE
CONV_WIDTH = 3
CONV_HALO = 8
F_PAD = 128
C_ROWS = 8

V7X_SUBLANES = 8
V7X_VMEM_LIMIT_BYTES = 56 * 1024 * 1024

FFN_TILE = 512
PROJ_TILE = 512
MERGE_TILE = 512
ATT_Q_TILE = 256
ATT_KV_TILE = 256
SSM_STEPS = 64


def _sigmoid(x):
    return 0.5 * jnp.tanh(0.5 * x) + 0.5


def _rmsnorm(x, g):
    return x * lax.rsqrt(jnp.mean(x * x, axis=-1, keepdims=True) + EPS) * g


def _dot(a, b):
    return jnp.dot(a, b, preferred_element_type=F32)


def _const_spec(shape):
    zeros = (0,) * len(shape)
    return pl.BlockSpec(shape, lambda *_: zeros, pipeline_mode=pl.Buffered(1))


def _params(*semantics):
    return pltpu.CompilerParams(dimension_semantics=semantics,
                                vmem_limit_bytes=V7X_VMEM_LIMIT_BYTES)


def _ffn_kernel(*refs, ple, final):
    h_ref, g_ref, wg_ref, wu_ref, wd_ref = refs[:5]
    rest = refs[5:]
    x = h_ref[...]
    xn = _rmsnorm(x, g_ref[...]).astype(BF16)
    a = _dot(xn, wg_ref[...])
    b = _dot(xn, wu_ref[...])
    act = (a * _sigmoid(a) * b).astype(BF16)
    y = x + 0.5 * _dot(act, wd_ref[...])
    if ple:
        p_ref, gp_ref, wpg_ref, wpp_ref = rest[:4]
        rest = rest[4:]
        yn = _rmsnorm(y, gp_ref[...]).astype(BF16)
        gate = _sigmoid(_dot(yn, wpg_ref[...]))
        y = y + gate * _dot(p_ref[...].astype(BF16), wpp_ref[...])
    if final:
        gf_ref = rest[0]
        rest = rest[1:]
        y = _rmsnorm(y, gf_ref[...])
    (o_ref,) = rest
    o_ref[...] = y


def _ffn(h, g, wg, wu, wd, ple_args=None, final_g=None):
    T, D = h.shape
    F = wg.shape[1]
    tm = min(FFN_TILE, T)
    row = lambda i: (i, 0)
    in_specs = [pl.BlockSpec((tm, D), row), _const_spec((1, D)), _const_spec((D, F)),
                _const_spec((D, F)), _const_spec((F, D))]
    args = [h, g, wg, wu, wd]
    if ple_args is not None:
        p, gp, wpg, wpp = ple_args
        P = p.shape[1]
        in_specs += [pl.BlockSpec((tm, P), row), _const_spec((1, D)), _const_spec((D, D)),
                     _const_spec((P, D))]
        args += [p, gp, wpg, wpp]
    if final_g is not None:
        in_specs.append(_const_spec((1, D)))
        args.append(final_g)
    return pl.pallas_call(
        functools.partial(_ffn_kernel, ple=ple_args is not None, final=final_g is not None),
        grid=(T // tm,),
        in_specs=in_specs,
        out_specs=pl.BlockSpec((tm, D), row),
        out_shape=jax.ShapeDtypeStruct((T, D), F32),
        compiler_params=_params("parallel"),
        name="ffn",
    )(*args)


def _proj_kernel(h_ref, g_ref, w_ref, fb_ref, pw_ref, ps_ref, cw_ref,
                 q_ref, k_ref, v_ref, ccol_ref, crow_ref, pool_ref, xs_ref, conv_ref,
                 carry_ref, xp_ext, zc_ext, *, tm):
    s_idx = pl.program_id(1)
    W = BRANCH_WIDTH

    @pl.when(s_idx == 0)
    def _():
        carry_ref[...] = jnp.zeros_like(carry_ref)
        xp_ext[0:POOL_HALO, :] = jnp.zeros((POOL_HALO, W), F32)
        zc_ext[0:CONV_HALO, :] = jnp.zeros((CONV_HALO, W), F32)

    u = _rmsnorm(h_ref[...], g_ref[...]).astype(BF16)
    z = _dot(u, w_ref[...])
    q_ref[...] = (z[:, 0:W] * (ATT_HEAD_DIM ** -0.5)).astype(BF16)
    k_ref[...] = z[:, W:2 * W].astype(BF16)
    v_ref[...] = z[:, 2 * W:3 * W].astype(BF16)
    xp = z[:, 3 * W:4 * W]
    xs_ref[...] = z[:, 4 * W:5 * W].astype(BF16)
    cb = z[:, 5 * W:6 * W]
    zc = z[:, 6 * W:7 * W] * z[:, 7 * W:8 * W]
    zf = z[:, 8 * W:8 * W + F_PAD]

    t = zf + fb_ref[...]
    lf = jnp.minimum(t, 0.0) - jnp.log1p(jnp.exp(-jnp.abs(t)))
    rows = lax.broadcasted_iota(jnp.int32, (tm, tm), 0)
    cols = lax.broadcasted_iota(jnp.int32, (tm, tm), 1)
    tri = jnp.where(rows >= cols, 1.0, 0.0).astype(BF16)
    hi = lf.astype(BF16)
    r1 = lf - hi.astype(F32)
    mid = r1.astype(BF16)
    lo = (r1 - mid.astype(F32)).astype(BF16)
    c = _dot(tri, hi) + _dot(tri, mid) + _dot(tri, lo) + carry_ref[...]
    carry_ref[...] = c[tm - 1:tm, :]
    ccol_ref[...] = c
    crow_ref[...] = c.T[0:C_ROWS, :]

    xp_ext[POOL_HALO:POOL_HALO + tm, :] = xp
    half = W // 2

    def shifted(k, lo_lane):
        return xp_ext[POOL_HALO - k:POOL_HALO - k + tm, lo_lane:lo_lane + half]

    def window(lo_lane, n):
        acc = shifted(0, lo_lane)
        for k in range(1, n):
            acc = acc + shifted(k, lo_lane)
        return acc

    lane = lax.broadcasted_iota(jnp.int32, (tm, half), 1)
    pos = s_idx * tm + lax.broadcasted_iota(jnp.int32, (tm, half), 0)
    first = lane < POOL_GROUP_DIM
    halves = []
    for hidx in range(2):
        w_small, w_big = POOL_WINDOWS[2 * hidx], POOL_WINDOWS[2 * hidx + 1]
        lo_lane = hidx * half
        sum_small = window(lo_lane, w_small)
        sum_big = sum_small
        for k in range(w_small, w_big):
            sum_big = sum_big + shifted(k, lo_lane)
        wsum = jnp.where(first, sum_small, sum_big)
        win = jnp.where(first, w_small, w_big)
        count = jnp.minimum(pos + 1, win).astype(F32)
        halves.append(wsum / count - shifted(0, lo_lane))
    pooled = jnp.concatenate(halves, axis=1).astype(BF16)
    pool_ref[...] = (_dot(pooled, pw_ref[...]) * ps_ref[...]).astype(BF16)
    xp_ext[0:POOL_HALO, :] = xp_ext[tm:tm + POOL_HALO, :]

    zc_ext[CONV_HALO:CONV_HALO + tm, :] = zc
    y = cw_ref[CONV_WIDTH - 1:CONV_WIDTH, :] * zc
    for j in range(CONV_WIDTH - 1):
        back = CONV_WIDTH - 1 - j
        y = y + cw_ref[j:j + 1, :] * zc_ext[CONV_HALO - back:CONV_HALO - back + tm, :]
    conv_ref[...] = (cb * y).astype(BF16)
    zc_ext[0:CONV_HALO, :] = zc_ext[tm:tm + CONV_HALO, :]


def _proj(h3, g, w, fb, pw, ps, cw):
    B, S, D = h3.shape
    W = BRANCH_WIDTH
    tm = min(PROJ_TILE, S)
    tok = lambda b, s: (b, s, 0)
    act = jax.ShapeDtypeStruct((B, S, W), BF16)
    act_spec = pl.BlockSpec((None, tm, W), tok)
    return pl.pallas_call(
        functools.partial(_proj_kernel, tm=tm),
        grid=(B, S // tm),
        in_specs=[pl.BlockSpec((None, tm, D), tok), _const_spec((1, D)),
                  _const_spec(w.shape), _const_spec((1, F_PAD)), _const_spec((W, W)),
                  _const_spec((1, W)), _const_spec((CONV_WIDTH, W))],
        out_specs=[act_spec, act_spec, act_spec,
                   pl.BlockSpec((None, tm, F_PAD), tok),
                   pl.BlockSpec((None, C_ROWS, tm), lambda b, s: (b, 0, s)),
                   act_spec,
                   pl.BlockSpec((tm, W), lambda b, s: (s, b)),
                   act_spec],
        out_shape=[act, act, act,
                   jax.ShapeDtypeStruct((B, S, F_PAD), F32),
                   jax.ShapeDtypeStruct((B, C_ROWS, S), F32),
                   act,
                   jax.ShapeDtypeStruct((S, B * W), BF16),
                   act],
        scratch_shapes=[pltpu.VMEM((1, F_PAD), F32),
                        pltpu.VMEM((POOL_HALO + tm, W), F32),
                        pltpu.VMEM((CONV_HALO + tm, W), F32)],
        compiler_params=_params("arbitrary", "arbitrary"),
        name="proj",
    )(h3, g, w, fb, pw, ps, cw)


def _att_kernel(q_ref, k_ref, v_ref, ccol_ref, crow_ref, o_ref, *, tq, tk):
    qi = pl.program_id(1)
    W = BRANCH_WIDTH
    q = q_ref[...]
    lane = lax.broadcasted_iota(jnp.int32, (tq, W), 1)
    rows = lax.broadcasted_iota(jnp.int32, (tq, tk), 0)
    cols = lax.broadcasted_iota(jnp.int32, (tq, tk), 1)
    n_full = (qi * tq) // tk
    out = jnp.zeros((tq, W), F32)
    for h in range(ATT_HEADS):
        in_head = (lane >= h * ATT_HEAD_DIM) & (lane < (h + 1) * ATT_HEAD_DIM)
        qh = jnp.where(in_head, q, jnp.zeros_like(q))
        c_t = ccol_ref[:, h:h + 1]

        def scores(j):
            start = pl.multiple_of(j * tk, tk)
            kj = k_ref[pl.ds(start, tk), :]
            s = lax.dot_general(qh, kj, (((1,), (1,)), ((), ())), preferred_element_type=F32)
            return s + (c_t - crow_ref[h:h + 1, pl.ds(start, tk)]), start

        def update(s, start, carry):
            m, l, acc = carry
            m_new = jnp.maximum(m, jnp.max(s, axis=-1, keepdims=True))
            alpha = jnp.exp(m - m_new)
            p = jnp.exp(s - m_new)
            l = alpha * l + jnp.sum(p, axis=-1, keepdims=True)
            acc = alpha * acc + _dot(p.astype(BF16), v_ref[pl.ds(start, tk), :])
            return m_new, l, acc

        def body(j, carry):
            s, start = scores(j)
            return update(s, start, carry)

        init = (jnp.full((tq, 1), NEG_INF, F32), jnp.zeros((tq, 1), F32),
                jnp.zeros((tq, W), F32))
        carry = lax.fori_loop(0, n_full, body, init)
        s, start = scores(n_full)
        s = jnp.where(cols <= rows, s, NEG_INF)
        m, l, acc = update(s, start, carry)
        out = jnp.where(in_head, acc / l, out)
    o_ref[...] = out.astype(o_ref.dtype)


def _attention(q, k, v, ccol, crow):
    B, S, W = q.shape
    tq = min(ATT_Q_TILE, S)
    tk = min(ATT_KV_TILE, S)
    assert tq == tk
    tok = lambda b, i: (b, i, 0)
    seq = lambda b, i: (b, 0, 0)
    return pl.pallas_call(
        functools.partial(_att_kernel, tq=tq, tk=tk),
        grid=(B, S // tq),
        in_specs=[pl.BlockSpec((None, tq, W), tok),
                  pl.BlockSpec((None, S, W), seq),
                  pl.BlockSpec((None, S, W), seq),
                  pl.BlockSpec((None, tq, F_PAD), tok),
                  pl.BlockSpec((None, C_ROWS, S), seq)],
        out_specs=pl.BlockSpec((None, tq, W), tok),
        out_shape=jax.ShapeDtypeStruct((B, S, W), BF16),
        compiler_params=_params("parallel", "arbitrary"),
        name="attention",
    )(q, k, v, ccol, crow)


def _ssm_kernel(u_ref, bd_ref, lr_ref, li_ref, cd_ref, d_ref, wglu_ref, o_ref,
                state_ref, x_ref, *, steps, batch):
    N = SSM_CHANNELS
    W = BRANCH_WIDTH

    @pl.when(pl.program_id(0) == 0)
    def _():
        state_ref[...] = jnp.zeros_like(state_ref)

    u = u_ref[...]
    x_ref[...] = _dot(u, bd_ref[...])
    lr = jnp.broadcast_to(lr_ref[...], (batch, N))
    li = jnp.broadcast_to(li_ref[...], (batch, N))

    def step(t, carry):
        sr, si = carry
        r0 = pl.multiple_of(t * batch, batch)
        nr = lr * sr - li * si + x_ref[pl.ds(r0, batch), 0:N]
        ni = lr * si + li * sr + x_ref[pl.ds(r0, batch), N:2 * N]
        x_ref[pl.ds(r0, batch), 0:N] = nr
        x_ref[pl.ds(r0, batch), N:2 * N] = ni
        return nr, ni

    sr, si = lax.fori_loop(0, steps, step, (state_ref[:, 0:N], state_ref[:, N:2 * N]))
    state_ref[:, 0:N] = sr
    state_ref[:, N:2 * N] = si

    y = _dot(x_ref[...].astype(BF16), cd_ref[...]) + d_ref[...] * u.astype(F32)
    yg = _dot(y.astype(BF16), wglu_ref[...])
    o_ref[...] = (yg[:, 0:W] * _sigmoid(yg[:, W:2 * W])).astype(o_ref.dtype)


def _ssm(xs_tm, batch, bd, lr, li, cd, d, wglu):
    R, W = xs_tm.shape
    S = R // batch
    steps = min(SSM_STEPS, S)
    rows = steps * batch
    N = SSM_CHANNELS
    return pl.pallas_call(
        functools.partial(_ssm_kernel, steps=steps, batch=batch),
        grid=(S // steps,),
        in_specs=[pl.BlockSpec((rows, W), lambda i: (i, 0)), _const_spec((W, 2 * N)),
                  _const_spec((1, N)), _const_spec((1, N)), _const_spec((2 * N, W)),
                  _const_spec((1, W)), _const_spec((W, 2 * W))],
        out_specs=pl.BlockSpec((rows, W), lambda i: (i, 0)),
        out_shape=jax.ShapeDtypeStruct((R, W), BF16),
        scratch_shapes=[pltpu.VMEM((batch, 2 * N), F32), pltpu.VMEM((rows, 2 * N), F32)],
        compiler_params=_params("arbitrary"),
        name="ssm",
    )(xs_tm, bd, lr, li, cd, d, wglu)


def _merge_kernel(h_ref, g_ref, wgate_ref, att_ref, pool_ref, ssm_ref, conv_ref,
                  wbr_ref, wout_ref, o_ref):
    D = h_ref.shape[-1]
    x = h_ref[...]
    u = _rmsnorm(x, g_ref[...]).astype(BF16)
    merged = None
    for n, y_ref in enumerate((att_ref, pool_ref, ssm_ref, conv_ref)):
        gate = _sigmoid(_dot(u, wgate_ref[:, n * D:(n + 1) * D]))
        term = gate * _dot(y_ref[...], wbr_ref[n])
        merged = term if merged is None else merged + term
    o_ref[...] = x + _dot(merged.astype(BF16), wout_ref[...])


def _merge(h3, g, wgate, y_att, y_pool, y_ssm_tm, y_conv, wbr, wout):
    B, S, D = h3.shape
    W = BRANCH_WIDTH
    tm = min(MERGE_TILE, S)
    tok = lambda b, s: (b, s, 0)
    act_spec = pl.BlockSpec((None, tm, W), tok)
    return pl.pallas_call(
        _merge_kernel,
        grid=(B, S // tm),
        in_specs=[pl.BlockSpec((None, tm, D), tok), _const_spec((1, D)),
                  _const_spec((D, N_BRANCH * D)), act_spec, act_spec,
                  pl.BlockSpec((tm, W), lambda b, s: (s, b)),
                  act_spec, _const_spec((N_BRANCH, W, D)), _const_spec((D, D))],
        out_specs=pl.BlockSpec((None, tm, D), tok),
        out_shape=jax.ShapeDtypeStruct((B, S, D), F32),
        compiler_params=_params("parallel", "parallel"),
        name="merge",
    )(h3, g, wgate, y_att, y_pool, y_ssm_tm, y_conv, wbr, wout)


def _block_diag(blocks):
    G, a, b = blocks.shape
    eye = jnp.eye(G, dtype=blocks.dtype)
    return (eye[:, None, :, None] * blocks[:, :, None, :]).reshape(G * a, G * b)


def _ssm_params(lam_re, lam_im, log_dt, b_re, b_im, c_re, c_im):
    dt = jnp.exp(log_dt)[:, None]
    lr = jnp.minimum(lam_re, -1e-4)
    li = lam_im
    mag = jnp.exp(lr * dt)
    lbr = mag * jnp.cos(li * dt)
    lbi = mag * jnp.sin(li * dt)
    den = lr * lr + li * li
    cr = ((lbr - 1.0) * lr + lbi * li) / den
    ci = (lbi * lr - (lbr - 1.0) * li) / den
    bbr = cr[..., None] * b_re - ci[..., None] * b_im
    bbi = cr[..., None] * b_im + ci[..., None] * b_re
    bd = jnp.concatenate([_block_diag(bbr.transpose(0, 2, 1)),
                          _block_diag(bbi.transpose(0, 2, 1))], axis=1)
    cd = jnp.concatenate([_block_diag(c_re.transpose(0, 2, 1)),
                          _block_diag(-c_im.transpose(0, 2, 1))], axis=0)
    return bd.astype(BF16), lbr.reshape(1, -1), lbi.reshape(1, -1), cd.astype(BF16)


def kernel(x, p, norm_g, ffn_w_gate, ffn_w_up, ffn_w_down, w_in, f_bias, pool_w, pool_scale,
           ssm_lam_re, ssm_lam_im, ssm_log_dt, ssm_b_re, ssm_b_im, ssm_c_re, ssm_c_im, ssm_d,
           ssm_w_glu, conv_w, w_branch, w_out, ple_w_gate, ple_w_proj, final_g):
    B, S, D = x.shape
    depth = norm_g.shape[0]
    T = B * S
    W = BRANCH_WIDTH
    assert B == V7X_SUBLANES, "the SSM keeps one batch row per sublane"
    n_mix = 3 * W + ATT_HEADS
    mix_cols = 8 * W + ATT_HEADS
    h = x.reshape(T, D)
    for i in range(depth):
        g = norm_g[i].reshape(-1, 1, D)
        h = _ffn(h, g[0], ffn_w_gate[i, 0].astype(BF16), ffn_w_up[i, 0].astype(BF16),
                 ffn_w_down[i, 0].astype(BF16))

        wi = w_in[i]
        w_mix = jnp.concatenate([wi[:, 0:3 * W], wi[:, n_mix:mix_cols], wi[:, 3 * W:n_mix],
                                 jnp.zeros((D, F_PAD - ATT_HEADS), F32)], axis=1).astype(BF16)
        fb = jnp.zeros((1, F_PAD), F32).at[0, 0:ATT_HEADS].set(f_bias[i])
        h3 = h.reshape(B, S, D)
        q, k, v, ccol, crow, y_pool, xs_tm, y_conv = _proj(
            h3, g[1], w_mix, fb, _block_diag(pool_w[i]).astype(BF16),
            pool_scale[i].reshape(1, W), conv_w[i])
        y_att = _attention(q, k, v, ccol, crow)
        bd, lr, li, cd = _ssm_params(ssm_lam_re[i], ssm_lam_im[i], ssm_log_dt[i], ssm_b_re[i],
                                     ssm_b_im[i], ssm_c_re[i], ssm_c_im[i])
        y_ssm_tm = _ssm(xs_tm.reshape(S * B, W), B, bd, lr, li, cd, ssm_d[i].reshape(1, W),
                        ssm_w_glu[i].astype(BF16))
        h3 = _merge(h3, g[1], wi[:, mix_cols:].astype(BF16), y_att, y_pool,
                    y_ssm_tm.reshape(S, B * W), y_conv, w_branch[i].astype(BF16),
                    w_out[i].astype(BF16))
        h = h3.reshape(T, D)

        h = _ffn(h, g[2], ffn_w_gate[i, 1].astype(BF16), ffn_w_up[i, 1].astype(BF16),
                 ffn_w_down[i, 1].astype(BF16),
                 ple_args=(p[i].reshape(T, -1), g[3], ple_w_gate[i].astype(BF16),
                           ple_w_proj[i].astype(BF16)),
                 final_g=final_g.reshape(1, D) if i == depth - 1 else None)
    return h.reshape(B, S, D)
```

```python
import functools

import jax
import jax.numpy as jnp
from jax import lax
from jax.experimental import pallas as pl
from jax.experimental.pallas import tpu as pltpu

F32 = jnp.float32
BF16 = jnp.bfloat16

EPS = 1e-6
NEG_INF = -1e30
LOG2E = 1.4426950408889634

ATT_HEADS = 4
ATT_HEAD_DIM = 64
BRANCH_WIDTH = 256
N_BRANCH = 4
POOL_WINDOWS = (2, 4, 8, 16)
POOL_GROUP_DIM = 64
POOL_HALO = 16
SSM_GROUPS = 16
SSM_GROUP_DIM = 16
SSM_STATE = 64
SSM_CHANNELS = SSM_GROUPS * SSM_STATE
CONV_WIDTH = 3
CONV_HALO = 8
F_PAD = 128
C_ROWS = 8

V7X_SUBLANES = 8
V7X_VMEM_LIMIT_BYTES = 56 * 1024 * 1024

FFN_TILE = 512
PROJ_TILE = 512
MERGE_TILE = 512
ATT_TILE = 512
SSM_STEPS = 64


def _sigmoid(x):
    return 0.5 * jnp.tanh(0.5 * x) + 0.5


def _rmsnorm(x, g):
    return x * lax.rsqrt(jnp.mean(x * x, axis=-1, keepdims=True) + EPS) * g


def _dot(a, b):
    return jnp.dot(a, b, preferred_element_type=F32)


def _const_spec(shape):
    zeros = (0,) * len(shape)
    return pl.BlockSpec(shape, lambda *_: zeros, pipeline_mode=pl.Buffered(1))


def _params(*semantics):
    return pltpu.CompilerParams(dimension_semantics=semantics,
                                vmem_limit_bytes=V7X_VMEM_LIMIT_BYTES)


def _ffn_kernel(*refs, ple, final):
    h_ref, g_ref, wg_ref, wu_ref, wd_ref = refs[:5]
    rest = refs[5:]
    x = h_ref[...]
    xn = _rmsnorm(x, g_ref[...]).astype(BF16)
    a = _dot(xn, wg_ref[...])
    b = _dot(xn, wu_ref[...])
    act = (a * _sigmoid(a) * b).astype(BF16)
    y = x + 0.5 * _dot(act, wd_ref[...])
    if ple:
        p_ref, gp_ref, wpg_ref, wpp_ref = rest[:4]
        rest = rest[4:]
        yn = _rmsnorm(y, gp_ref[...]).astype(BF16)
        gate = _sigmoid(_dot(yn, wpg_ref[...]))
        y = y + gate * _dot(p_ref[...].astype(BF16), wpp_ref[...])
    if final:
        gf_ref = rest[0]
        rest = rest[1:]
        y = _rmsnorm(y, gf_ref[...])
    (o_ref,) = rest
    o_ref[...] = y


def _ffn(h, g, wg, wu, wd, ple_args=None, final_g=None):
    T, D = h.shape
    F = wg.shape[1]
    tm = min(FFN_TILE, T)
    row = lambda i: (i, 0)
    in_specs = [pl.BlockSpec((tm, D), row), _const_spec((1, D)), _const_spec((D, F)),
                _const_spec((D, F)), _const_spec((F, D))]
    args = [h, g, wg, wu, wd]
    if ple_args is not None:
        p, gp, wpg, wpp = ple_args
        P = p.shape[1]
        in_specs += [pl.BlockSpec((tm, P), row), _const_spec((1, D)), _const_spec((D, D)),
                     _const_spec((P, D))]
        args += [p, gp, wpg, wpp]
    if final_g is not None:
        in_specs.append(_const_spec((1, D)))
        args.append(final_g)
    return pl.pallas_call(
        functools.partial(_ffn_kernel, ple=ple_args is not None, final=final_g is not None),
        grid=(T // tm,),
        in_specs=in_specs,
        out_specs=pl.BlockSpec((tm, D), row),
        out_shape=jax.ShapeDtypeStruct((T, D), F32),
        compiler_params=_params("parallel"),
        name="ffn",
    )(*args)


def _proj_kernel(h_ref, g_ref, w_ref, fb_ref, pw_ref, ps_ref, cw_ref,
                 qt_ref, k_ref, vt_ref, ccol_ref, crow_ref, pool_ref, xs_ref, conv_ref,
                 carry_ref, xp_ext, zc_ext, *, tm):
    s_idx = pl.program_id(1)
    W = BRANCH_WIDTH

    @pl.when(s_idx == 0)
    def _():
        carry_ref[...] = jnp.zeros_like(carry_ref)
        xp_ext[0:POOL_HALO, :] = jnp.zeros((POOL_HALO, W), F32)
        zc_ext[0:CONV_HALO, :] = jnp.zeros((CONV_HALO, W), F32)

    u = _rmsnorm(h_ref[...], g_ref[...]).astype(BF16)
    z = _dot(u, w_ref[...])
    qt_ref[...] = (z[:, 0:W] * (LOG2E * ATT_HEAD_DIM ** -0.5)).T.astype(BF16)
    k_ref[...] = z[:, W:2 * W].astype(BF16)
    vt_ref[...] = z[:, 2 * W:3 * W].T.astype(BF16)
    xp = z[:, 3 * W:4 * W]
    xs_ref[...] = z[:, 4 * W:5 * W].astype(BF16)
    cb = z[:, 5 * W:6 * W]
    zc = z[:, 6 * W:7 * W] * z[:, 7 * W:8 * W]
    zf = z[:, 8 * W:8 * W + F_PAD]

    t = (zf + fb_ref[...]).T[0:C_ROWS, :]
    lf = jnp.minimum(t, 0.0) - jnp.log1p(jnp.exp(-jnp.abs(t)))
    hi = lf.astype(BF16).astype(F32)
    mid = (lf - hi).astype(BF16).astype(F32)
    lo = lf - hi - mid
    pieces = jnp.concatenate([hi, mid, lo, jnp.zeros_like(lo)], axis=0).astype(BF16)
    rows = lax.broadcasted_iota(jnp.int32, (tm, tm), 0)
    cols = lax.broadcasted_iota(jnp.int32, (tm, tm), 1)
    upper = jnp.where(rows <= cols, 1.0, 0.0).astype(BF16)
    sums = _dot(pieces, upper)
    c = (sums[0:C_ROWS] + sums[C_ROWS:2 * C_ROWS] + sums[2 * C_ROWS:3 * C_ROWS]
         + carry_ref[:, 0:1])
    carry_ref[...] = jnp.broadcast_to(c[:, tm - 1:tm], carry_ref.shape)
    c2 = c * LOG2E
    crow_ref[...] = c2
    ccol_ref[...] = jnp.concatenate(
        [c2, jnp.zeros((F_PAD - C_ROWS, tm), F32)], axis=0).T

    xp_ext[POOL_HALO:POOL_HALO + tm, :] = xp
    half = W // 2

    def shifted(k, lo_lane):
        return xp_ext[POOL_HALO - k:POOL_HALO - k + tm, lo_lane:lo_lane + half]

    def window(lo_lane, n):
        acc = shifted(0, lo_lane)
        for k in range(1, n):
            acc = acc + shifted(k, lo_lane)
        return acc

    lane = lax.broadcasted_iota(jnp.int32, (tm, half), 1)
    pos = s_idx * tm + lax.broadcasted_iota(jnp.int32, (tm, half), 0)
    first = lane < POOL_GROUP_DIM
    halves = []
    for hidx in range(2):
        w_small, w_big = POOL_WINDOWS[2 * hidx], POOL_WINDOWS[2 * hidx + 1]
        lo_lane = hidx * half
        sum_small = window(lo_lane, w_small)
        sum_big = sum_small
        for k in range(w_small, w_big):
            sum_big = sum_big + shifted(k, lo_lane)
        wsum = jnp.where(first, sum_small, sum_big)
        win = jnp.where(first, w_small, w_big)
        count = jnp.minimum(pos + 1, win).astype(F32)
        halves.append(wsum / count - shifted(0, lo_lane))
    pooled = jnp.concatenate(halves, axis=1).astype(BF16)
    pool_ref[...] = (_dot(pooled, pw_ref[...]) * ps_ref[...]).astype(BF16)
    xp_ext[0:POOL_HALO, :] = xp_ext[tm:tm + POOL_HALO, :]

    zc_ext[CONV_HALO:CONV_HALO + tm, :] = zc
    y = cw_ref[CONV_WIDTH - 1:CONV_WIDTH, :] * zc
    for j in range(CONV_WIDTH - 1):
        back = CONV_WIDTH - 1 - j
        y = y + cw_ref[j:j + 1, :] * zc_ext[CONV_HALO - back:CONV_HALO - back + tm, :]
    conv_ref[...] = (cb * y).astype(BF16)
    zc_ext[0:CONV_HALO, :] = zc_ext[tm:tm + CONV_HALO, :]


def _proj(h3, g, w, fb, pw, ps, cw):
    B, S, D = h3.shape
    W = BRANCH_WIDTH
    tm = min(PROJ_TILE, S)
    tok = lambda b, s: (b, s, 0)
    act = jax.ShapeDtypeStruct((B, S, W), BF16)
    act_spec = pl.BlockSpec((None, tm, W), tok)
    act_t = jax.ShapeDtypeStruct((B, W, S), BF16)
    act_t_spec = pl.BlockSpec((None, W, tm), lambda b, s: (b, 0, s))
    return pl.pallas_call(
        functools.partial(_proj_kernel, tm=tm),
        grid=(B, S // tm),
        in_specs=[pl.BlockSpec((None, tm, D), tok), _const_spec((1, D)),
                  _const_spec(w.shape), _const_spec((1, F_PAD)), _const_spec((W, W)),
                  _const_spec((1, W)), _const_spec((CONV_WIDTH, W))],
        out_specs=[act_t_spec, act_spec, act_t_spec,
                   pl.BlockSpec((None, tm, F_PAD), tok),
                   pl.BlockSpec((None, C_ROWS, tm), lambda b, s: (b, 0, s)),
                   act_spec, act_spec, act_spec],
        out_shape=[act_t, act, act_t,
                   jax.ShapeDtypeStruct((B, S, F_PAD), F32),
                   jax.ShapeDtypeStruct((B, C_ROWS, S), F32),
                   act, act, act],
        scratch_shapes=[pltpu.VMEM((C_ROWS, F_PAD), F32),
                        pltpu.VMEM((POOL_HALO + tm, W), F32),
                        pltpu.VMEM((CONV_HALO + tm, W), F32)],
        compiler_params=_params("arbitrary", "arbitrary"),
        name="proj",
    )(h3, g, w, fb, pw, ps, cw)


def _att_kernel(qt_ref, k_ref, vt_ref, crow_ref, ccol_ref, o_ref, *, tile):
    qi = pl.program_id(1)
    W = BRANCH_WIDTH
    Dh = ATT_HEAD_DIM
    qt = qt_ref[...]
    feat = lax.broadcasted_iota(jnp.int32, (W, tile), 0)
    qts = [jnp.where((feat >= h * Dh) & (feat < (h + 1) * Dh), qt, jnp.zeros_like(qt))
           for h in range(ATT_HEADS)]
    c_t = [crow_ref[h:h + 1, :] for h in range(ATT_HEADS)]
    kv_pos = lax.broadcasted_iota(jnp.int32, (tile, tile), 0)
    q_pos = lax.broadcasted_iota(jnp.int32, (tile, tile), 1)

    def block(j, carry, diagonal):
        start = pl.multiple_of(j * tile, tile)
        kj = k_ref[pl.ds(start, tile), :]
        new = []
        for h in range(ATT_HEADS):
            m, l, acc = carry[h]
            s = _dot(kj, qts[h]) + (c_t[h] - ccol_ref[pl.ds(start, tile), h:h + 1])
            if diagonal:
                s = jnp.where(kv_pos <= q_pos, s, NEG_INF)
            m_new = jnp.maximum(m, jnp.max(s, axis=0, keepdims=True))
            alpha = jnp.exp2(m - m_new)
            p = jnp.exp2(s - m_new)
            l = alpha * l + jnp.sum(p, axis=0, keepdims=True)
            vth = vt_ref[h * Dh:(h + 1) * Dh, pl.ds(start, tile)]
            acc = alpha * acc + _dot(vth, p.astype(BF16))
            new.append((m_new, l, acc))
        return tuple(new)

    init = tuple((jnp.full((1, tile), NEG_INF, F32), jnp.zeros((1, tile), F32),
                  jnp.zeros((Dh, tile), F32)) for _ in range(ATT_HEADS))
    carry = lax.fori_loop(0, qi, lambda j, c: block(j, c, False), init)
    carry = block(qi, carry, True)
    out_t = jnp.concatenate([acc / l for (_, l, acc) in carry], axis=0)
    o_ref[...] = out_t.T.astype(o_ref.dtype)


def _attention(qt, k, vt, crow, ccol):
    B, S, W = k.shape
    tile = min(ATT_TILE, S)
    seq = lambda b, i: (b, 0, 0)
    qcol = lambda b, i: (b, 0, i)
    return pl.pallas_call(
        functools.partial(_att_kernel, tile=tile),
        grid=(B, S // tile),
        in_specs=[pl.BlockSpec((None, W, tile), qcol),
                  pl.BlockSpec((None, S, W), seq),
                  pl.BlockSpec((None, W, S), seq),
                  pl.BlockSpec((None, C_ROWS, tile), qcol),
                  pl.BlockSpec((None, S, F_PAD), seq)],
        out_specs=pl.BlockSpec((None, tile, W), lambda b, i: (b, i, 0)),
        out_shape=jax.ShapeDtypeStruct((B, S, W), BF16),
        compiler_params=_params("parallel", "arbitrary"),
        name="attention",
    )(qt, k, vt, crow, ccol)


def _ssm_kernel(u_ref, perm_ref, perm_t_ref, bd_ref, lr_ref, li_ref, cd_ref, d_ref, wglu_ref,
                o_ref, state_ref, x_ref, *, steps, batch):
    N = SSM_CHANNELS
    W = BRANCH_WIDTH
    rows = steps * batch

    @pl.when(pl.program_id(0) == 0)
    def _():
        state_ref[...] = jnp.zeros_like(state_ref)

    u = _dot(perm_ref[...], u_ref[...].reshape(rows, W)).astype(BF16)
    x_ref[...] = _dot(u, bd_ref[...])
    lr = jnp.broadcast_to(lr_ref[...], (batch, N))
    li = jnp.broadcast_to(li_ref[...], (batch, N))

    def step(t, carry):
        sr, si = carry
        r0 = pl.multiple_of(t * batch, batch)
        nr = lr * sr - li * si + x_ref[pl.ds(r0, batch), 0:N]
        ni = lr * si + li * sr + x_ref[pl.ds(r0, batch), N:2 * N]
        x_ref[pl.ds(r0, batch), 0:N] = nr
        x_ref[pl.ds(r0, batch), N:2 * N] = ni
        return nr, ni

    sr, si = lax.fori_loop(0, steps, step, (state_ref[:, 0:N], state_ref[:, N:2 * N]))
    state_ref[:, 0:N] = sr
    state_ref[:, N:2 * N] = si

    y = _dot(x_ref[...].astype(BF16), cd_ref[...]) + d_ref[...] * u.astype(F32)
    yg = _dot(y.astype(BF16), wglu_ref[...])
    out = (yg[:, 0:W] * _sigmoid(yg[:, W:2 * W])).astype(BF16)
    o_ref[...] = _dot(perm_t_ref[...], out).astype(o_ref.dtype).reshape(batch, steps, W)


def _ssm(xs, bd, lr, li, cd, d, wglu):
    B, S, W = xs.shape
    steps = min(SSM_STEPS, S)
    rows = steps * B
    N = SSM_CHANNELS
    src = (jnp.arange(rows) % B) * steps + jnp.arange(rows) // B
    perm = (src[:, None] == jnp.arange(rows)[None, :]).astype(BF16)
    chunk = lambda i: (0, i, 0)
    return pl.pallas_call(
        functools.partial(_ssm_kernel, steps=steps, batch=B),
        grid=(S // steps,),
        in_specs=[pl.BlockSpec((B, steps, W), chunk), _const_spec((rows, rows)),
                  _const_spec((rows, rows)), _const_spec((W, 2 * N)),
                  _const_spec((1, N)), _const_spec((1, N)), _const_spec((2 * N, W)),
                  _const_spec((1, W)), _const_spec((W, 2 * W))],
        out_specs=pl.BlockSpec((B, steps, W), chunk),
        out_shape=jax.ShapeDtypeStruct((B, S, W), BF16),
        scratch_shapes=[pltpu.VMEM((B, 2 * N), F32), pltpu.VMEM((rows, 2 * N), F32)],
        compiler_params=_params("arbitrary"),
        name="ssm",
    )(xs, perm, perm.T, bd, lr, li, cd, d, wglu)


def _merge_kernel(h_ref, g_ref, wgate_ref, att_ref, pool_ref, ssm_ref, conv_ref,
                  wbr_ref, wout_ref, o_ref):
    D = h_ref.shape[-1]
    x = h_ref[...]
    u = _rmsnorm(x, g_ref[...]).astype(BF16)
    merged = None
    for n, y_ref in enumerate((att_ref, pool_ref, ssm_ref, conv_ref)):
        gate = _sigmoid(_dot(u, wgate_ref[:, n * D:(n + 1) * D]))
        term = gate * _dot(y_ref[...], wbr_ref[n])
        merged = term if merged is None else merged + term
    o_ref[...] = x + _dot(merged.astype(BF16), wout_ref[...])


def _merge(h3, g, wgate, y_att, y_pool, y_ssm, y_conv, wbr, wout):
    B, S, D = h3.shape
    W = BRANCH_WIDTH
    tm = min(MERGE_TILE, S)
    tok = lambda b, s: (b, s, 0)
    act_spec = pl.BlockSpec((None, tm, W), tok)
    return pl.pallas_call(
        _merge_kernel,
        grid=(B, S // tm),
        in_specs=[pl.BlockSpec((None, tm, D), tok), _const_spec((1, D)),
                  _const_spec((D, N_BRANCH * D)), act_spec, act_spec, act_spec, act_spec,
                  _const_spec((N_BRANCH, W, D)), _const_spec((D, D))],
        out_specs=pl.BlockSpec((None, tm, D), tok),
        out_shape=jax.ShapeDtypeStruct((B, S, D), F32),
        compiler_params=_params("parallel", "parallel"),
        name="merge",
    )(h3, g, wgate, y_att, y_pool, y_ssm, y_conv, wbr, wout)


def _block_diag(blocks):
    G, a, b = blocks.shape
    eye = jnp.eye(G, dtype=blocks.dtype)
    return (eye[:, None, :, None] * blocks[:, :, None, :]).reshape(G * a, G * b)


def _ssm_params(lam_re, lam_im, log_dt, b_re, b_im, c_re, c_im):
    dt = jnp.exp(log_dt)[:, None]
    lr = jnp.minimum(lam_re, -1e-4)
    li = lam_im
    mag = jnp.exp(lr * dt)
    lbr = mag * jnp.cos(li * dt)
    lbi = mag * jnp.sin(li * dt)
    den = lr * lr + li * li
    cr = ((lbr - 1.0) * lr + lbi * li) / den
    ci = (lbi * lr - (lbr - 1.0) * li) / den
    bbr = cr[..., None] * b_re - ci[..., None] * b_im
    bbi = cr[..., None] * b_im + ci[..., None] * b_re
    bd = jnp.concatenate([_block_diag(bbr.transpose(0, 2, 1)),
                          _block_diag(bbi.transpose(0, 2, 1))], axis=1)
    cd = jnp.concatenate([_block_diag(c_re.transpose(0, 2, 1)),
                          _block_diag(-c_im.transpose(0, 2, 1))], axis=0)
    return bd.astype(BF16), lbr.reshape(1, -1), lbi.reshape(1, -1), cd.astype(BF16)


def kernel(x, p, norm_g, ffn_w_gate, ffn_w_up, ffn_w_down, w_in, f_bias, pool_w, pool_scale,
           ssm_lam_re, ssm_lam_im, ssm_log_dt, ssm_b_re, ssm_b_im, ssm_c_re, ssm_c_im, ssm_d,
           ssm_w_glu, conv_w, w_branch, w_out, ple_w_gate, ple_w_proj, final_g):
    B, S, D = x.shape
    depth = norm_g.shape[0]
    T = B * S
    W = BRANCH_WIDTH
    assert B == V7X_SUBLANES, "the SSM keeps one batch row per sublane"
    n_mix = 3 * W + ATT_HEADS
    mix_cols = 8 * W + ATT_HEADS
    h = x.reshape(T, D)
    for i in range(depth):
        g = norm_g[i].reshape(-1, 1, D)
        h = _ffn(h, g[0], ffn_w_gate[i, 0].astype(BF16), ffn_w_up[i, 0].astype(BF16),
                 ffn_w_down[i, 0].astype(BF16))

        wi = w_in[i]
        w_mix = jnp.concatenate([wi[:, 0:3 * W], wi[:, n_mix:mix_cols], wi[:, 3 * W:n_mix],
                                 jnp.zeros((D, F_PAD - ATT_HEADS), F32)], axis=1).astype(BF16)
        fb = jnp.zeros((1, F_PAD), F32).at[0, 0:ATT_HEADS].set(f_bias[i])
        h3 = h.reshape(B, S, D)
        qt, k, vt, ccol, crow, y_pool, xs, y_conv = _proj(
            h3, g[1], w_mix, fb, _block_diag(pool_w[i]).astype(BF16),
            pool_scale[i].reshape(1, W), conv_w[i])
        y_att = _attention(qt, k, vt, crow, ccol)
        bd, lr, li, cd = _ssm_params(ssm_lam_re[i], ssm_lam_im[i], ssm_log_dt[i], ssm_b_re[i],
                                     ssm_b_im[i], ssm_c_re[i], ssm_c_im[i])
        y_ssm = _ssm(xs, bd, lr, li, cd, ssm_d[i].reshape(1, W), ssm_w_glu[i].astype(BF16))
        h3 = _merge(h3, g[1], wi[:, mix_cols:].astype(BF16), y_att, y_pool, y_ssm, y_conv,
                    w_branch[i].astype(BF16), w_out[i].astype(BF16))
        h = h3.reshape(T, D)

        h = _ffn(h, g[2], ffn_w_gate[i, 1].astype(BF16), ffn_w_up[i, 1].astype(BF16),
                 ffn_w_down[i, 1].astype(BF16),
                 ple_args=(p[i].reshape(T, -1), g[3], ple_w_gate[i].astype(BF16),
                           ple_w_proj[i].astype(BF16)),
                 final_g=final_g.reshape(1, D) if i == depth - 1 else None)
    return h.reshape(B, S, D)
```

```python
import functools

import jax
import jax.numpy as jnp
from jax import lax
from jax.experimental import pallas as pl
from jax.experimental.pallas import tpu as pltpu

F32 = jnp.float32
BF16 = jnp.bfloat16

EPS = 1e-6
NEG_INF = -1e30
LOG2E = 1.4426950408889634

ATT_HEADS = 4
ATT_HEAD_DIM = 64
BRANCH_WIDTH = 256
N_BRANCH = 4
POOL_WINDOWS = (2, 4, 8, 16)
POOL_GROUP_DIM = 64
POOL_HALO = 16
SSM_GROUPS = 16
SSM_GROUP_DIM = 16
SSM_STATE = 64
SSM_CHANNELS = SSM_GROUPS * SSM_STATE
CONV_WIDTH = 3
CONV_HALO = 8
F_PAD = 128
C_ROWS = 8

V7X_SUBLANES = 8
V7X_VMEM_LIMIT_BYTES = 56 * 1024 * 1024

FFN_TILE = 512
PROJ_TILE = 512
MERGE_TILE = 512
ATT_TILE = 512
SSM_STEPS = 64
REGROUP_ROWS = 256


def _sigmoid(x):
    return 0.5 * jnp.tanh(0.5 * x) + 0.5


def _rmsnorm(x, g):
    return x * lax.rsqrt(jnp.mean(x * x, axis=-1, keepdims=True) + EPS) * g


def _dot(a, b):
    return jnp.dot(a, b, preferred_element_type=F32)


def _const_spec(shape):
    zeros = (0,) * len(shape)
    return pl.BlockSpec(shape, lambda *_: zeros, pipeline_mode=pl.Buffered(1))


def _params(*semantics):
    return pltpu.CompilerParams(dimension_semantics=semantics,
                                vmem_limit_bytes=V7X_VMEM_LIMIT_BYTES)


def _ffn_kernel(*refs, ple, final):
    h_ref, g_ref, wg_ref, wu_ref, wd_ref = refs[:5]
    rest = refs[5:]
    x = h_ref[...]
    xn = _rmsnorm(x, g_ref[...]).astype(BF16)
    a = _dot(xn, wg_ref[...])
    b = _dot(xn, wu_ref[...])
    act = (a * _sigmoid(a) * b).astype(BF16)
    y = x + 0.5 * _dot(act, wd_ref[...])
    if ple:
        p_ref, gp_ref, wpg_ref, wpp_ref = rest[:4]
        rest = rest[4:]
        yn = _rmsnorm(y, gp_ref[...]).astype(BF16)
        gate = _sigmoid(_dot(yn, wpg_ref[...]))
        y = y + gate * _dot(p_ref[...].astype(BF16), wpp_ref[...])
    if final:
        gf_ref = rest[0]
        rest = rest[1:]
        y = _rmsnorm(y, gf_ref[...])
    (o_ref,) = rest
    o_ref[...] = y


def _picked_spec(shape, index):
    lead = len(index)
    block = (None,) * lead + tuple(shape[lead:])
    full = tuple(index) + (0,) * (len(shape) - lead)
    return pl.BlockSpec(block, lambda *_: full, pipeline_mode=pl.Buffered(1))


def _ffn(h, g, wg, wu, wd, which, ple_args=None, final_g=None):
    T, D = h.shape
    F = wg.shape[-1]
    tm = min(FFN_TILE, T)
    row = lambda i: (i, 0)
    in_specs = [pl.BlockSpec((tm, D), row), _const_spec((1, D)), _picked_spec(wg.shape, which),
                _picked_spec(wu.shape, which), _picked_spec(wd.shape, which)]
    args = [h, g, wg, wu, wd]
    if ple_args is not None:
        p, gp, wpg, wpp = ple_args
        P = p.shape[-1]
        layer = which[0]
        in_specs += [pl.BlockSpec((None, tm, P), lambda i: (layer, i, 0)), _const_spec((1, D)),
                     _picked_spec(wpg.shape, (layer,)), _picked_spec(wpp.shape, (layer,))]
        args += [p, gp, wpg, wpp]
    if final_g is not None:
        in_specs.append(_const_spec((1, D)))
        args.append(final_g)
    return pl.pallas_call(
        functools.partial(_ffn_kernel, ple=ple_args is not None, final=final_g is not None),
        grid=(T // tm,),
        in_specs=in_specs,
        out_specs=pl.BlockSpec((tm, D), row),
        out_shape=jax.ShapeDtypeStruct((T, D), F32),
        compiler_params=_params("parallel"),
        name="ffn",
    )(*args)


def _proj_kernel(h_ref, g_ref, w_ref, fb_ref, pw_ref, ps_ref, cw_ref,
                 qt_ref, k_ref, vt_ref, ccol_ref, crow_ref, pool_ref, xs_ref, conv_ref,
                 carry_ref, xp_ext, zc_ext, *, tm):
    s_idx = pl.program_id(1)
    W = BRANCH_WIDTH

    @pl.when(s_idx == 0)
    def _():
        carry_ref[...] = jnp.zeros_like(carry_ref)
        xp_ext[0:POOL_HALO, :] = jnp.zeros((POOL_HALO, W), F32)
        zc_ext[0:CONV_HALO, :] = jnp.zeros((CONV_HALO, W), F32)

    u = _rmsnorm(h_ref[...], g_ref[...]).astype(BF16)
    z = _dot(u, w_ref[...])
    qt_ref[...] = (z[:, 0:W] * (LOG2E * ATT_HEAD_DIM ** -0.5)).T.astype(BF16)
    k_ref[...] = z[:, W:2 * W].astype(BF16)
    vt_ref[...] = z[:, 2 * W:3 * W].T.astype(BF16)
    xp = z[:, 3 * W:4 * W]
    xs_ref[...] = z[:, 4 * W:5 * W].astype(BF16)
    cb = z[:, 5 * W:6 * W]
    zc = z[:, 6 * W:7 * W] * z[:, 7 * W:8 * W]
    zf = z[:, 8 * W:8 * W + F_PAD]

    t = (zf + fb_ref[...]).T[0:C_ROWS, :]
    lf = jnp.minimum(t, 0.0) - jnp.log1p(jnp.exp(-jnp.abs(t)))
    hi = lf.astype(BF16).astype(F32)
    mid = (lf - hi).astype(BF16).astype(F32)
    lo = lf - hi - mid
    pieces = jnp.concatenate([hi, mid, lo, jnp.zeros_like(lo)], axis=0).astype(BF16)
    rows = lax.broadcasted_iota(jnp.int32, (tm, tm), 0)
    cols = lax.broadcasted_iota(jnp.int32, (tm, tm), 1)
    upper = jnp.where(rows <= cols, 1.0, 0.0).astype(BF16)
    sums = _dot(pieces, upper)
    c = (sums[0:C_ROWS] + sums[C_ROWS:2 * C_ROWS] + sums[2 * C_ROWS:3 * C_ROWS]
         + carry_ref[:, 0:1])
    carry_ref[...] = jnp.broadcast_to(c[:, tm - 1:tm], carry_ref.shape)
    c2 = c * LOG2E
    crow_ref[...] = c2
    ccol_ref[...] = jnp.concatenate(
        [c2, jnp.zeros((F_PAD - C_ROWS, tm), F32)], axis=0).T

    xp_ext[POOL_HALO:POOL_HALO + tm, :] = xp
    half = W // 2

    def shifted(k, lo_lane):
        return xp_ext[POOL_HALO - k:POOL_HALO - k + tm, lo_lane:lo_lane + half]

    def window(lo_lane, n):
        acc = shifted(0, lo_lane)
        for k in range(1, n):
            acc = acc + shifted(k, lo_lane)
        return acc

    lane = lax.broadcasted_iota(jnp.int32, (tm, half), 1)
    pos = s_idx * tm + lax.broadcasted_iota(jnp.int32, (tm, half), 0)
    first = lane < POOL_GROUP_DIM
    halves = []
    for hidx in range(2):
        w_small, w_big = POOL_WINDOWS[2 * hidx], POOL_WINDOWS[2 * hidx + 1]
        lo_lane = hidx * half
        sum_small = window(lo_lane, w_small)
        sum_big = sum_small
        for k in range(w_small, w_big):
            sum_big = sum_big + shifted(k, lo_lane)
        wsum = jnp.where(first, sum_small, sum_big)
        win = jnp.where(first, w_small, w_big)
        count = jnp.minimum(pos + 1, win).astype(F32)
        halves.append(wsum / count - shifted(0, lo_lane))
    pooled = jnp.concatenate(halves, axis=1).astype(BF16)
    pool_ref[...] = (_dot(pooled, pw_ref[...]) * ps_ref[...]).astype(BF16)
    xp_ext[0:POOL_HALO, :] = xp_ext[tm:tm + POOL_HALO, :]

    zc_ext[CONV_HALO:CONV_HALO + tm, :] = zc
    y = cw_ref[CONV_WIDTH - 1:CONV_WIDTH, :] * zc
    for j in range(CONV_WIDTH - 1):
        back = CONV_WIDTH - 1 - j
        y = y + cw_ref[j:j + 1, :] * zc_ext[CONV_HALO - back:CONV_HALO - back + tm, :]
    conv_ref[...] = (cb * y).astype(BF16)
    zc_ext[0:CONV_HALO, :] = zc_ext[tm:tm + CONV_HALO, :]


def _proj(h3, g, w, layer, fb, pw, ps, cw):
    B, S, D = h3.shape
    W = BRANCH_WIDTH
    tm = min(PROJ_TILE, S)
    tok = lambda b, s: (b, s, 0)
    act = jax.ShapeDtypeStruct((B, S, W), BF16)
    act_spec = pl.BlockSpec((None, tm, W), tok)
    act_t = jax.ShapeDtypeStruct((B, W, S), BF16)
    act_t_spec = pl.BlockSpec((None, W, tm), lambda b, s: (b, 0, s))
    return pl.pallas_call(
        functools.partial(_proj_kernel, tm=tm),
        grid=(B, S // tm),
        in_specs=[pl.BlockSpec((None, tm, D), tok), _const_spec((1, D)),
                  _picked_spec(w.shape, (layer,)), _const_spec((1, F_PAD)), _const_spec((W, W)),
                  _const_spec((1, W)), _const_spec((CONV_WIDTH, W))],
        out_specs=[act_t_spec, act_spec, act_t_spec,
                   pl.BlockSpec((None, tm, F_PAD), tok),
                   pl.BlockSpec((None, C_ROWS, tm), lambda b, s: (b, 0, s)),
                   act_spec, act_spec, act_spec],
        out_shape=[act_t, act, act_t,
                   jax.ShapeDtypeStruct((B, S, F_PAD), F32),
                   jax.ShapeDtypeStruct((B, C_ROWS, S), F32),
                   act, act, act],
        scratch_shapes=[pltpu.VMEM((C_ROWS, F_PAD), F32),
                        pltpu.VMEM((POOL_HALO + tm, W), F32),
                        pltpu.VMEM((CONV_HALO + tm, W), F32)],
        compiler_params=_params("arbitrary", "arbitrary"),
        name="proj",
    )(h3, g, w, fb, pw, ps, cw)


def _att_kernel(qt_ref, k_ref, vt_ref, crow_ref, ccol_ref, o_ref, *, tile):
    qi = pl.program_id(1)
    W = BRANCH_WIDTH
    Dh = ATT_HEAD_DIM
    qt = qt_ref[...]
    feat = lax.broadcasted_iota(jnp.int32, (W, tile), 0)
    qts = [jnp.where((feat >= h * Dh) & (feat < (h + 1) * Dh), qt, jnp.zeros_like(qt))
           for h in range(ATT_HEADS)]
    c_t = [crow_ref[h:h + 1, :] for h in range(ATT_HEADS)]
    kv_pos = lax.broadcasted_iota(jnp.int32, (tile, tile), 0)
    q_pos = lax.broadcasted_iota(jnp.int32, (tile, tile), 1)

    def block(j, carry, diagonal):
        start = pl.multiple_of(j * tile, tile)
        kj = k_ref[pl.ds(start, tile), :]
        new = []
        for h in range(ATT_HEADS):
            m, l, acc = carry[h]
            t = _dot(kj, qts[h]) - ccol_ref[pl.ds(start, tile), h:h + 1]
            if diagonal:
                t = jnp.where(kv_pos <= q_pos, t, NEG_INF)
            m_new = jnp.maximum(m, jnp.max(t, axis=0, keepdims=True) + c_t[h])
            alpha = jnp.exp2(m - m_new)
            p = jnp.exp2(t + (c_t[h] - m_new))
            l = alpha * l + jnp.sum(p, axis=0, keepdims=True)
            vth = vt_ref[h * Dh:(h + 1) * Dh, pl.ds(start, tile)]
            acc = alpha * acc + _dot(vth, p.astype(BF16))
            new.append((m_new, l, acc))
        return tuple(new)

    init = tuple((jnp.full((1, tile), NEG_INF, F32), jnp.zeros((1, tile), F32),
                  jnp.zeros((Dh, tile), F32)) for _ in range(ATT_HEADS))
    carry = lax.fori_loop(0, qi, lambda j, c: block(j, c, False), init)
    carry = block(qi, carry, True)
    out_t = jnp.concatenate([acc / l for (_, l, acc) in carry], axis=0)
    o_ref[...] = out_t.T.astype(o_ref.dtype)


def _attention(qt, k, vt, crow, ccol):
    B, S, W = k.shape
    tile = min(ATT_TILE, S)
    seq = lambda b, i: (b, 0, 0)
    qcol = lambda b, i: (b, 0, i)
    return pl.pallas_call(
        functools.partial(_att_kernel, tile=tile),
        grid=(B, S // tile),
        in_specs=[pl.BlockSpec((None, W, tile), qcol),
                  pl.BlockSpec((None, S, W), seq),
                  pl.BlockSpec((None, W, S), seq),
                  pl.BlockSpec((None, C_ROWS, tile), qcol),
                  pl.BlockSpec((None, S, F_PAD), seq)],
        out_specs=pl.BlockSpec((None, tile, W), lambda b, i: (b, i, 0)),
        out_shape=jax.ShapeDtypeStruct((B, S, W), BF16),
        compiler_params=_params("parallel", "arbitrary"),
        name="attention",
    )(qt, k, vt, crow, ccol)


def _ssm_kernel(u_ref, perm_ref, perm_t_ref, bd_ref, lr_ref, li_ref, cd_ref, d_ref, wglu_ref,
                o_ref, state_ref, x_ref, *, steps, batch):
    N = SSM_CHANNELS
    W = BRANCH_WIDTH
    rows = steps * batch

    @pl.when(pl.program_id(0) == 0)
    def _():
        state_ref[...] = jnp.zeros_like(state_ref)

    u = _dot(perm_ref[...], u_ref[...].reshape(rows, W)).astype(BF16)
    x_ref[...] = _dot(u, bd_ref[...])
    lr = jnp.broadcast_to(lr_ref[...], (batch, N))
    li = jnp.broadcast_to(li_ref[...], (batch, N))

    def step(t, carry):
        sr, si = carry
        r0 = pl.multiple_of(t * batch, batch)
        nr = lr * sr - li * si + x_ref[pl.ds(r0, batch), 0:N]
        ni = lr * si + li * sr + x_ref[pl.ds(r0, batch), N:2 * N]
        x_ref[pl.ds(r0, batch), 0:N] = nr
        x_ref[pl.ds(r0, batch), N:2 * N] = ni
        return nr, ni

    sr, si = lax.fori_loop(0, steps, step, (state_ref[:, 0:N], state_ref[:, N:2 * N]))
    state_ref[:, 0:N] = sr
    state_ref[:, N:2 * N] = si

    y = _dot(x_ref[...].astype(BF16), cd_ref[...]) + d_ref[...] * u.astype(F32)
    yg = _dot(y.astype(BF16), wglu_ref[...])
    out = (yg[:, 0:W] * _sigmoid(yg[:, W:2 * W])).astype(BF16)
    o_ref[...] = _dot(perm_t_ref[...], out).astype(o_ref.dtype).reshape(batch, steps, W)


def _ssm(xs, bd, lr, li, cd, d, wglu):
    B, S, W = xs.shape
    steps = min(SSM_STEPS, S)
    rows = steps * B
    N = SSM_CHANNELS
    src = (jnp.arange(rows) % B) * steps + jnp.arange(rows) // B
    perm = (src[:, None] == jnp.arange(rows)[None, :]).astype(BF16)
    chunk = lambda i: (0, i, 0)
    return pl.pallas_call(
        functools.partial(_ssm_kernel, steps=steps, batch=B),
        grid=(S // steps,),
        in_specs=[pl.BlockSpec((B, steps, W), chunk), _const_spec((rows, rows)),
                  _const_spec((rows, rows)), _const_spec((W, 2 * N)),
                  _const_spec((1, N)), _const_spec((1, N)), _const_spec((2 * N, W)),
                  _const_spec((1, W)), _const_spec((W, 2 * W))],
        out_specs=pl.BlockSpec((B, steps, W), chunk),
        out_shape=jax.ShapeDtypeStruct((B, S, W), BF16),
        scratch_shapes=[pltpu.VMEM((B, 2 * N), F32), pltpu.VMEM((rows, 2 * N), F32)],
        compiler_params=_params("arbitrary"),
        name="ssm",
    )(xs, perm, perm.T, bd, lr, li, cd, d, wglu)


def _merge_kernel(h_ref, g_ref, wgate_ref, att_ref, pool_ref, ssm_ref, conv_ref,
                  wbr_ref, wout_ref, o_ref):
    D = h_ref.shape[-1]
    x = h_ref[...]
    u = _rmsnorm(x, g_ref[...]).astype(BF16)
    merged = None
    for n, y_ref in enumerate((att_ref, pool_ref, ssm_ref, conv_ref)):
        gate = _sigmoid(_dot(u, wgate_ref[:, n * D:(n + 1) * D]))
        term = gate * _dot(y_ref[...], wbr_ref[n])
        merged = term if merged is None else merged + term
    o_ref[...] = x + _dot(merged.astype(BF16), wout_ref[...])


def _merge(h3, g, wgate, layer, y_att, y_pool, y_ssm, y_conv, wbr, wout):
    B, S, D = h3.shape
    W = BRANCH_WIDTH
    tm = min(MERGE_TILE, S)
    tok = lambda b, s: (b, s, 0)
    act_spec = pl.BlockSpec((None, tm, W), tok)
    return pl.pallas_call(
        _merge_kernel,
        grid=(B, S // tm),
        in_specs=[pl.BlockSpec((None, tm, D), tok), _const_spec((1, D)),
                  _picked_spec(wgate.shape, (layer,)), act_spec, act_spec, act_spec, act_spec,
                  _const_spec((N_BRANCH, W, D)), _const_spec((D, D))],
        out_specs=pl.BlockSpec((None, tm, D), tok),
        out_shape=jax.ShapeDtypeStruct((B, S, D), F32),
        compiler_params=_params("parallel", "parallel"),
        name="merge",
    )(h3, g, wgate, y_att, y_pool, y_ssm, y_conv, wbr, wout)


def _regroup_kernel(w_ref, mix_ref, gate_ref):
    W = BRANCH_WIDTH
    n_mix = 3 * W + ATT_HEADS
    mix_cols = 8 * W + ATT_HEADS
    w = w_ref[...]
    rows = w.shape[0]
    mix_ref[:, 0:3 * W] = w[:, 0:3 * W].astype(BF16)
    mix_ref[:, 3 * W:8 * W] = w[:, n_mix:mix_cols].astype(BF16)
    mix_ref[:, 8 * W:8 * W + F_PAD] = jnp.concatenate(
        [w[:, 3 * W:n_mix], jnp.zeros((rows, F_PAD - ATT_HEADS), F32)], axis=1).astype(BF16)
    gate_ref[...] = w[:, mix_cols:].astype(BF16)


def _regroup_w_in(w_in):
    depth, D, cols = w_in.shape
    W = BRANCH_WIDTH
    n_gate = cols - (8 * W + ATT_HEADS)
    rows = min(REGROUP_ROWS, D)
    blk = lambda i, r: (i, r, 0)
    return pl.pallas_call(
        _regroup_kernel,
        grid=(depth, D // rows),
        in_specs=[pl.BlockSpec((None, rows, cols), blk)],
        out_specs=[pl.BlockSpec((None, rows, 8 * W + F_PAD), blk),
                   pl.BlockSpec((None, rows, n_gate), blk)],
        out_shape=[jax.ShapeDtypeStruct((depth, D, 8 * W + F_PAD), BF16),
                   jax.ShapeDtypeStruct((depth, D, n_gate), BF16)],
        compiler_params=_params("parallel", "parallel"),
        name="regroup_w_in",
    )(w_in)


def _block_diag(blocks):
    G, a, b = blocks.shape
    eye = jnp.eye(G, dtype=blocks.dtype)
    return (eye[:, None, :, None] * blocks[:, :, None, :]).reshape(G * a, G * b)


def _ssm_params(lam_re, lam_im, log_dt, b_re, b_im, c_re, c_im):
    dt = jnp.exp(log_dt)[:, None]
    lr = jnp.minimum(lam_re, -1e-4)
    li = lam_im
    mag = jnp.exp(lr * dt)
    lbr = mag * jnp.cos(li * dt)
    lbi = mag * jnp.sin(li * dt)
    den = lr * lr + li * li
    cr = ((lbr - 1.0) * lr + lbi * li) / den
    ci = (lbi * lr - (lbr - 1.0) * li) / den
    bbr = cr[..., None] * b_re - ci[..., None] * b_im
    bbi = cr[..., None] * b_im + ci[..., None] * b_re
    bd = jnp.concatenate([_block_diag(bbr.transpose(0, 2, 1)),
                          _block_diag(bbi.transpose(0, 2, 1))], axis=1)
    cd = jnp.concatenate([_block_diag(c_re.transpose(0, 2, 1)),
                          _block_diag(-c_im.transpose(0, 2, 1))], axis=0)
    return bd.astype(BF16), lbr.reshape(1, -1), lbi.reshape(1, -1), cd.astype(BF16)


def kernel(x, p, norm_g, ffn_w_gate, ffn_w_up, ffn_w_down, w_in, f_bias, pool_w, pool_scale,
           ssm_lam_re, ssm_lam_im, ssm_log_dt, ssm_b_re, ssm_b_im, ssm_c_re, ssm_c_im, ssm_d,
           ssm_w_glu, conv_w, w_branch, w_out, ple_w_gate, ple_w_proj, final_g):
    B, S, D = x.shape
    depth = norm_g.shape[0]
    T = B * S
    W = BRANCH_WIDTH
    assert B == V7X_SUBLANES, "the SSM keeps one batch row per sublane"
    h = x.reshape(T, D)
    wg_all, wu_all, wd_all = (w.astype(BF16) for w in (ffn_w_gate, ffn_w_up, ffn_w_down))
    wpg_all, wpp_all = ple_w_gate.astype(BF16), ple_w_proj.astype(BF16)
    p3 = p.reshape(depth, T, p.shape[-1])
    w_mix_all, w_gate_all = _regroup_w_in(w_in)
    for i in range(depth):
        g = norm_g[i].reshape(-1, 1, D)
        h = _ffn(h, g[0], wg_all, wu_all, wd_all, (i, 0))

        fb = jnp.zeros((1, F_PAD), F32).at[0, 0:ATT_HEADS].set(f_bias[i])
        h3 = h.reshape(B, S, D)
        qt, k, vt, ccol, crow, y_pool, xs, y_conv = _proj(
            h3, g[1], w_mix_all, i, fb, _block_diag(pool_w[i]).astype(BF16),
            pool_scale[i].reshape(1, W), conv_w[i])
        y_att = _attention(qt, k, vt, crow, ccol)
        bd, lr, li, cd = _ssm_params(ssm_lam_re[i], ssm_lam_im[i], ssm_log_dt[i], ssm_b_re[i],
                                     ssm_b_im[i], ssm_c_re[i], ssm_c_im[i])
        y_ssm = _ssm(xs, bd, lr, li, cd, ssm_d[i].reshape(1, W), ssm_w_glu[i].astype(BF16))
        h3 = _merge(h3, g[1], w_gate_all, i, y_att, y_pool, y_ssm, y_conv,
                    w_branch[i].astype(BF16), w_out[i].astype(BF16))
        h = h3.reshape(T, D)

        h = _ffn(h, g[2], wg_all, wu_all, wd_all, (i, 1),
                 ple_args=(p3, g[3], wpg_all, wpp_all),
                 final_g=final_g.reshape(1, D) if i == depth - 1 else None)
    return h.reshape(B, S, D)
```

```python
import functools

import jax
import jax.numpy as jnp
from jax import lax
from jax.experimental import pallas as pl
from jax.experimental.pallas import tpu as pltpu

F32 = jnp.float32
BF16 = jnp.bfloat16

EPS = 1e-6
NEG_INF = -1e30
LOG2E = 1.4426950408889634

ATT_HEADS = 4
ATT_HEAD_DIM = 64
BRANCH_WIDTH = 256
N_BRANCH = 4
POOL_WINDOWS = (2, 4, 8, 16)
POOL_GROUP_DIM = 64
POOL_HALO = 16
SSM_GROUPS = 16
SSM_GROUP_DIM = 16
SSM_STATE = 64
SSM_CHANNELS = SSM_GROUPS * SSM_STATE
CONV_WIDTH = 3
CONV_HALO = 8
F_PAD = 128
C_ROWS = 8

V7X_SUBLANES = 8
V7X_VMEM_LIMIT_BYTES = 56 * 1024 * 1024

FFN_TILE = 512
PROJ_TILE = 512
MERGE_TILE = 512
ATT_TILE = 512
SSM_STEPS = 64
REGROUP_ROWS = 256


def _sigmoid(x):
    return 0.5 * jnp.tanh(0.5 * x) + 0.5


def _rmsnorm(x, g):
    return x * lax.rsqrt(jnp.mean(x * x, axis=-1, keepdims=True) + EPS) * g


def _dot(a, b):
    return jnp.dot(a, b, preferred_element_type=F32)


def _const_spec(shape):
    zeros = (0,) * len(shape)
    return pl.BlockSpec(shape, lambda *_: zeros, pipeline_mode=pl.Buffered(1))


def _params(*semantics):
    return pltpu.CompilerParams(dimension_semantics=semantics,
                                vmem_limit_bytes=V7X_VMEM_LIMIT_BYTES)


def _ffn_kernel(*refs, ple, final):
    h_ref, g_ref, wg_ref, wu_ref, wd_ref = refs[:5]
    rest = refs[5:]
    x = h_ref[...]
    xn = _rmsnorm(x, g_ref[...]).astype(BF16)
    a = _dot(xn, wg_ref[...])
    b = _dot(xn, wu_ref[...])
    act = (a * _sigmoid(a) * b).astype(BF16)
    y = x + 0.5 * _dot(act, wd_ref[...])
    if ple:
        p_ref, gp_ref, wpg_ref, wpp_ref = rest[:4]
        rest = rest[4:]
        yn = _rmsnorm(y, gp_ref[...]).astype(BF16)
        gate = _sigmoid(_dot(yn, wpg_ref[...]))
        y = y + gate * _dot(p_ref[...].astype(BF16), wpp_ref[...])
    if final:
        gf_ref = rest[0]
        rest = rest[1:]
        y = _rmsnorm(y, gf_ref[...])
    (o_ref,) = rest
    o_ref[...] = y


def _picked_spec(shape, index):
    lead = len(index)
    block = (None,) * lead + tuple(shape[lead:])
    full = tuple(index) + (0,) * (len(shape) - lead)
    return pl.BlockSpec(block, lambda *_: full, pipeline_mode=pl.Buffered(1))


def _ffn(h, g, wg, wu, wd, which, ple_args=None, final_g=None):
    T, D = h.shape
    F = wg.shape[-1]
    tm = min(FFN_TILE, T)
    row = lambda i: (i, 0)
    in_specs = [pl.BlockSpec((tm, D), row), _const_spec((1, D)), _picked_spec(wg.shape, which),
                _picked_spec(wu.shape, which), _picked_spec(wd.shape, which)]
    args = [h, g, wg, wu, wd]
    if ple_args is not None:
        p, gp, wpg, wpp = ple_args
        P = p.shape[-1]
        layer = which[0]
        in_specs += [pl.BlockSpec((None, tm, P), lambda i: (layer, i, 0)), _const_spec((1, D)),
                     _picked_spec(wpg.shape, (layer,)), _picked_spec(wpp.shape, (layer,))]
        args += [p, gp, wpg, wpp]
    if final_g is not None:
        in_specs.append(_const_spec((1, D)))
        args.append(final_g)
    return pl.pallas_call(
        functools.partial(_ffn_kernel, ple=ple_args is not None, final=final_g is not None),
        grid=(T // tm,),
        in_specs=in_specs,
        out_specs=pl.BlockSpec((tm, D), row),
        out_shape=jax.ShapeDtypeStruct((T, D), F32),
        compiler_params=_params("parallel"),
        name="ffn",
    )(*args)


def _proj_kernel(h_ref, g_ref, w_ref, fb_ref, pw_ref, ps_ref, cw_ref,
                 qt_ref, k_ref, vt_ref, ccol_ref, crow_ref, pool_ref, xs_ref, conv_ref,
                 carry_ref, xp_ext, zc_ext, *, tm):
    s_idx = pl.program_id(1)
    W = BRANCH_WIDTH

    @pl.when(s_idx == 0)
    def _():
        carry_ref[...] = jnp.zeros_like(carry_ref)
        xp_ext[0:POOL_HALO, :] = jnp.zeros((POOL_HALO, W), F32)
        zc_ext[0:CONV_HALO, :] = jnp.zeros((CONV_HALO, W), F32)

    u = _rmsnorm(h_ref[...], g_ref[...]).astype(BF16)
    z = _dot(u, w_ref[...])
    qt_ref[...] = (z[:, 0:W] * (LOG2E * ATT_HEAD_DIM ** -0.5)).T.astype(BF16)
    k_ref[...] = z[:, W:2 * W].astype(BF16)
    vt_ref[...] = z[:, 2 * W:3 * W].T.astype(BF16)
    xp = z[:, 3 * W:4 * W]
    xs_ref[...] = z[:, 4 * W:5 * W].astype(BF16)
    cb = z[:, 5 * W:6 * W]
    zc = z[:, 6 * W:7 * W] * z[:, 7 * W:8 * W]
    zf = z[:, 8 * W:8 * W + F_PAD]

    t = (zf + fb_ref[...]).T[0:C_ROWS, :]
    lf = jnp.minimum(t, 0.0) - jnp.log1p(jnp.exp(-jnp.abs(t)))
    hi = lf.astype(BF16).astype(F32)
    mid = (lf - hi).astype(BF16).astype(F32)
    lo = lf - hi - mid
    pieces = jnp.concatenate([hi, mid, lo, jnp.zeros_like(lo)], axis=0).astype(BF16)
    rows = lax.broadcasted_iota(jnp.int32, (tm, tm), 0)
    cols = lax.broadcasted_iota(jnp.int32, (tm, tm), 1)
    upper = jnp.where(rows <= cols, 1.0, 0.0).astype(BF16)
    sums = _dot(pieces, upper)
    c = (sums[0:C_ROWS] + sums[C_ROWS:2 * C_ROWS] + sums[2 * C_ROWS:3 * C_ROWS]
         + carry_ref[:, 0:1])
    carry_ref[...] = jnp.broadcast_to(c[:, tm - 1:tm], carry_ref.shape)
    c2 = c * LOG2E
    crow_ref[...] = c2
    ccol_ref[...] = jnp.concatenate(
        [c2, jnp.zeros((F_PAD - C_ROWS, tm), F32)], axis=0).T

    xp_ext[POOL_HALO:POOL_HALO + tm, :] = xp
    half = W // 2

    def shifted(k, lo_lane):
        return xp_ext[POOL_HALO - k:POOL_HALO - k + tm, lo_lane:lo_lane + half]

    def window(lo_lane, n):
        acc = shifted(0, lo_lane)
        for k in range(1, n):
            acc = acc + shifted(k, lo_lane)
        return acc

    lane = lax.broadcasted_iota(jnp.int32, (tm, half), 1)
    pos = s_idx * tm + lax.broadcasted_iota(jnp.int32, (tm, half), 0)
    first = lane < POOL_GROUP_DIM
    halves = []
    for hidx in range(2):
        w_small, w_big = POOL_WINDOWS[2 * hidx], POOL_WINDOWS[2 * hidx + 1]
        lo_lane = hidx * half
        sum_small = window(lo_lane, w_small)
        sum_big = sum_small
        for k in range(w_small, w_big):
            sum_big = sum_big + shifted(k, lo_lane)
        wsum = jnp.where(first, sum_small, sum_big)
        win = jnp.where(first, w_small, w_big)
        count = jnp.minimum(pos + 1, win).astype(F32)
        halves.append(wsum / count - shifted(0, lo_lane))
    pooled = jnp.concatenate(halves, axis=1).astype(BF16)
    pool_ref[...] = (_dot(pooled, pw_ref[...]) * ps_ref[...]).astype(BF16)
    xp_ext[0:POOL_HALO, :] = xp_ext[tm:tm + POOL_HALO, :]

    zc_ext[CONV_HALO:CONV_HALO + tm, :] = zc
    y = cw_ref[CONV_WIDTH - 1:CONV_WIDTH, :] * zc
    for j in range(CONV_WIDTH - 1):
        back = CONV_WIDTH - 1 - j
        y = y + cw_ref[j:j + 1, :] * zc_ext[CONV_HALO - back:CONV_HALO - back + tm, :]
    conv_ref[...] = (cb * y).astype(BF16)
    zc_ext[0:CONV_HALO, :] = zc_ext[tm:tm + CONV_HALO, :]


def _proj(h3, g, w, layer, fb, pw, ps, cw):
    B, S, D = h3.shape
    W = BRANCH_WIDTH
    tm = min(PROJ_TILE, S)
    tok = lambda b, s: (b, s, 0)
    act = jax.ShapeDtypeStruct((B, S, W), BF16)
    act_spec = pl.BlockSpec((None, tm, W), tok)
    act_t = jax.ShapeDtypeStruct((B, W, S), BF16)
    act_t_spec = pl.BlockSpec((None, W, tm), lambda b, s: (b, 0, s))
    return pl.pallas_call(
        functools.partial(_proj_kernel, tm=tm),
        grid=(B, S // tm),
        in_specs=[pl.BlockSpec((None, tm, D), tok), _const_spec((1, D)),
                  _picked_spec(w.shape, (layer,)), _const_spec((1, F_PAD)), _const_spec((W, W)),
                  _const_spec((1, W)), _const_spec((CONV_WIDTH, W))],
        out_specs=[act_t_spec, act_spec, act_t_spec,
                   pl.BlockSpec((None, tm, F_PAD), tok),
                   pl.BlockSpec((None, C_ROWS, tm), lambda b, s: (b, 0, s)),
                   act_spec, act_spec, act_spec],
        out_shape=[act_t, act, act_t,
                   jax.ShapeDtypeStruct((B, S, F_PAD), F32),
                   jax.ShapeDtypeStruct((B, C_ROWS, S), F32),
                   act, act, act],
        scratch_shapes=[pltpu.VMEM((C_ROWS, F_PAD), F32),
                        pltpu.VMEM((POOL_HALO + tm, W), F32),
                        pltpu.VMEM((CONV_HALO + tm, W), F32)],
        compiler_params=_params("arbitrary", "arbitrary"),
        name="proj",
    )(h3, g, w, fb, pw, ps, cw)


def _att_kernel(qt_ref, k_ref, vt_ref, crow_ref, ccol_ref, o_ref, *, tile):
    qi = pl.program_id(1)
    W = BRANCH_WIDTH
    Dh = ATT_HEAD_DIM
    qt = qt_ref[...]
    feat = lax.broadcasted_iota(jnp.int32, (W, tile), 0)
    qts = [jnp.where((feat >= h * Dh) & (feat < (h + 1) * Dh), qt, jnp.zeros_like(qt))
           for h in range(ATT_HEADS)]
    c_t = [crow_ref[h:h + 1, :] for h in range(ATT_HEADS)]
    kv_pos = lax.broadcasted_iota(jnp.int32, (tile, tile), 0)
    q_pos = lax.broadcasted_iota(jnp.int32, (tile, tile), 1)

    def block(j, carry, diagonal):
        start = pl.multiple_of(j * tile, tile)
        kj = k_ref[pl.ds(start, tile), :]
        kq = [_dot(kj, qts[h]) for h in range(ATT_HEADS)]
        new = []
        for h in range(ATT_HEADS):
            m, l, acc = carry[h]
            t = kq[h] - ccol_ref[pl.ds(start, tile), h:h + 1]
            if diagonal:
                t = jnp.where(kv_pos <= q_pos, t, NEG_INF)
            m_new = jnp.maximum(m, jnp.max(t, axis=0, keepdims=True) + c_t[h])
            alpha = jnp.exp2(m - m_new)
            p = jnp.exp2(t + (c_t[h] - m_new))
            l = alpha * l + jnp.sum(p, axis=0, keepdims=True)
            vth = vt_ref[h * Dh:(h + 1) * Dh, pl.ds(start, tile)]
            acc = alpha * acc + _dot(vth, p.astype(BF16))
            new.append((m_new, l, acc))
        return tuple(new)

    init = tuple((jnp.full((1, tile), NEG_INF, F32), jnp.zeros((1, tile), F32),
                  jnp.zeros((Dh, tile), F32)) for _ in range(ATT_HEADS))
    carry = lax.fori_loop(0, qi, lambda j, c: block(j, c, False), init)
    carry = block(qi, carry, True)
    out_t = jnp.concatenate([acc / l for (_, l, acc) in carry], axis=0)
    o_ref[...] = out_t.T.astype(o_ref.dtype)


def _attention(qt, k, vt, crow, ccol):
    B, S, W = k.shape
    tile = min(ATT_TILE, S)
    seq = lambda b, i: (b, 0, 0)
    qcol = lambda b, i: (b, 0, i)
    return pl.pallas_call(
        functools.partial(_att_kernel, tile=tile),
        grid=(B, S // tile),
        in_specs=[pl.BlockSpec((None, W, tile), qcol),
                  pl.BlockSpec((None, S, W), seq),
                  pl.BlockSpec((None, W, S), seq),
                  pl.BlockSpec((None, C_ROWS, tile), qcol),
                  pl.BlockSpec((None, S, F_PAD), seq)],
        out_specs=pl.BlockSpec((None, tile, W), lambda b, i: (b, i, 0)),
        out_shape=jax.ShapeDtypeStruct((B, S, W), BF16),
        compiler_params=_params("parallel", "arbitrary"),
        name="attention",
    )(qt, k, vt, crow, ccol)


def _ssm_kernel(u_ref, perm_ref, perm_t_ref, bd_ref, lr_ref, li_ref, cd_ref, d_ref, wglu_ref,
                o_ref, state_ref, x_ref, *, steps, batch):
    N = SSM_CHANNELS
    W = BRANCH_WIDTH
    rows = steps * batch

    @pl.when(pl.program_id(0) == 0)
    def _():
        state_ref[...] = jnp.zeros_like(state_ref)

    u = _dot(perm_ref[...], u_ref[...].reshape(rows, W)).astype(BF16)
    x_ref[...] = _dot(u, bd_ref[...])
    lr = jnp.broadcast_to(lr_ref[...], (batch, N))
    li = jnp.broadcast_to(li_ref[...], (batch, N))

    def step(t, carry):
        sr, si = carry
        r0 = pl.multiple_of(t * batch, batch)
        nr = lr * sr - li * si + x_ref[pl.ds(r0, batch), 0:N]
        ni = lr * si + li * sr + x_ref[pl.ds(r0, batch), N:2 * N]
        x_ref[pl.ds(r0, batch), 0:N] = nr
        x_ref[pl.ds(r0, batch), N:2 * N] = ni
        return nr, ni

    sr, si = lax.fori_loop(0, steps, step, (state_ref[:, 0:N], state_ref[:, N:2 * N]))
    state_ref[:, 0:N] = sr
    state_ref[:, N:2 * N] = si

    y = _dot(x_ref[...].astype(BF16), cd_ref[...]) + d_ref[...] * u.astype(F32)
    yg = _dot(y.astype(BF16), wglu_ref[...])
    out = (yg[:, 0:W] * _sigmoid(yg[:, W:2 * W])).astype(BF16)
    o_ref[...] = _dot(perm_t_ref[...], out).astype(o_ref.dtype).reshape(batch, steps, W)


def _ssm(xs, bd, lr, li, cd, d, wglu):
    B, S, W = xs.shape
    steps = min(SSM_STEPS, S)
    rows = steps * B
    N = SSM_CHANNELS
    src = (jnp.arange(rows) % B) * steps + jnp.arange(rows) // B
    perm = (src[:, None] == jnp.arange(rows)[None, :]).astype(BF16)
    chunk = lambda i: (0, i, 0)
    return pl.pallas_call(
        functools.partial(_ssm_kernel, steps=steps, batch=B),
        grid=(S // steps,),
        in_specs=[pl.BlockSpec((B, steps, W), chunk), _const_spec((rows, rows)),
                  _const_spec((rows, rows)), _const_spec((W, 2 * N)),
                  _const_spec((1, N)), _const_spec((1, N)), _const_spec((2 * N, W)),
                  _const_spec((1, W)), _const_spec((W, 2 * W))],
        out_specs=pl.BlockSpec((B, steps, W), chunk),
        out_shape=jax.ShapeDtypeStruct((B, S, W), BF16),
        scratch_shapes=[pltpu.VMEM((B, 2 * N), F32), pltpu.VMEM((rows, 2 * N), F32)],
        compiler_params=_params("arbitrary"),
        name="ssm",
    )(xs, perm, perm.T, bd, lr, li, cd, d, wglu)


def _merge_kernel(h_ref, g_ref, wgate_ref, att_ref, pool_ref, ssm_ref, conv_ref,
                  wbr_ref, wout_ref, o_ref):
    D = h_ref.shape[-1]
    x = h_ref[...]
    u = _rmsnorm(x, g_ref[...]).astype(BF16)
    merged = None
    for n, y_ref in enumerate((att_ref, pool_ref, ssm_ref, conv_ref)):
        gate = _sigmoid(_dot(u, wgate_ref[:, n * D:(n + 1) * D]))
        term = gate * _dot(y_ref[...], wbr_ref[n])
        merged = term if merged is None else merged + term
    o_ref[...] = x + _dot(merged.astype(BF16), wout_ref[...])


def _merge(h3, g, wgate, layer, y_att, y_pool, y_ssm, y_conv, wbr, wout):
    B, S, D = h3.shape
    W = BRANCH_WIDTH
    tm = min(MERGE_TILE, S)
    tok = lambda b, s: (b, s, 0)
    act_spec = pl.BlockSpec((None, tm, W), tok)
    return pl.pallas_call(
        _merge_kernel,
        grid=(B, S // tm),
        in_specs=[pl.BlockSpec((None, tm, D), tok), _const_spec((1, D)),
                  _picked_spec(wgate.shape, (layer,)), act_spec, act_spec, act_spec, act_spec,
                  _const_spec((N_BRANCH, W, D)), _const_spec((D, D))],
        out_specs=pl.BlockSpec((None, tm, D), tok),
        out_shape=jax.ShapeDtypeStruct((B, S, D), F32),
        compiler_params=_params("parallel", "parallel"),
        name="merge",
    )(h3, g, wgate, y_att, y_pool, y_ssm, y_conv, wbr, wout)


def _regroup_kernel(w_ref, mix_ref, gate_ref):
    W = BRANCH_WIDTH
    n_mix = 3 * W + ATT_HEADS
    mix_cols = 8 * W + ATT_HEADS
    w = w_ref[...]
    rows = w.shape[0]
    mix_ref[:, 0:3 * W] = w[:, 0:3 * W].astype(BF16)
    mix_ref[:, 3 * W:8 * W] = w[:, n_mix:mix_cols].astype(BF16)
    mix_ref[:, 8 * W:8 * W + F_PAD] = jnp.concatenate(
        [w[:, 3 * W:n_mix], jnp.zeros((rows, F_PAD - ATT_HEADS), F32)], axis=1).astype(BF16)
    gate_ref[...] = w[:, mix_cols:].astype(BF16)


def _regroup_w_in(w_in):
    depth, D, cols = w_in.shape
    W = BRANCH_WIDTH
    n_gate = cols - (8 * W + ATT_HEADS)
    rows = min(REGROUP_ROWS, D)
    blk = lambda i, r: (i, r, 0)
    return pl.pallas_call(
        _regroup_kernel,
        grid=(depth, D // rows),
        in_specs=[pl.BlockSpec((None, rows, cols), blk)],
        out_specs=[pl.BlockSpec((None, rows, 8 * W + F_PAD), blk),
                   pl.BlockSpec((None, rows, n_gate), blk)],
        out_shape=[jax.ShapeDtypeStruct((depth, D, 8 * W + F_PAD), BF16),
                   jax.ShapeDtypeStruct((depth, D, n_gate), BF16)],
        compiler_params=_params("parallel", "parallel"),
        name="regroup_w_in",
    )(w_in)


def _block_diag(blocks):
    G, a, b = blocks.shape
    eye = jnp.eye(G, dtype=blocks.dtype)
    return (eye[:, None, :, None] * blocks[:, :, None, :]).reshape(G * a, G * b)


def _ssm_params(lam_re, lam_im, log_dt, b_re, b_im, c_re, c_im):
    dt = jnp.exp(log_dt)[:, None]
    lr = jnp.minimum(lam_re, -1e-4)
    li = lam_im
    mag = jnp.exp(lr * dt)
    lbr = mag * jnp.cos(li * dt)
    lbi = mag * jnp.sin(li * dt)
    den = lr * lr + li * li
    cr = ((lbr - 1.0) * lr + lbi * li) / den
    ci = (lbi * lr - (lbr - 1.0) * li) / den
    bbr = cr[..., None] * b_re - ci[..., None] * b_im
    bbi = cr[..., None] * b_im + ci[..., None] * b_re
    bd = jnp.concatenate([_block_diag(bbr.transpose(0, 2, 1)),
                          _block_diag(bbi.transpose(0, 2, 1))], axis=1)
    cd = jnp.concatenate([_block_diag(c_re.transpose(0, 2, 1)),
                          _block_diag(-c_im.transpose(0, 2, 1))], axis=0)
    return bd.astype(BF16), lbr.reshape(1, -1), lbi.reshape(1, -1), cd.astype(BF16)


def kernel(x, p, norm_g, ffn_w_gate, ffn_w_up, ffn_w_down, w_in, f_bias, pool_w, pool_scale,
           ssm_lam_re, ssm_lam_im, ssm_log_dt, ssm_b_re, ssm_b_im, ssm_c_re, ssm_c_im, ssm_d,
           ssm_w_glu, conv_w, w_branch, w_out, ple_w_gate, ple_w_proj, final_g):
    B, S, D = x.shape
    depth = norm_g.shape[0]
    T = B * S
    W = BRANCH_WIDTH
    assert B == V7X_SUBLANES, "the SSM keeps one batch row per sublane"
    h = x.reshape(T, D)
    wg_all, wu_all, wd_all = (w.astype(BF16) for w in (ffn_w_gate, ffn_w_up, ffn_w_down))
    wpg_all, wpp_all = ple_w_gate.astype(BF16), ple_w_proj.astype(BF16)
    p3 = p.reshape(depth, T, p.shape[-1])
    w_mix_all, w_gate_all = _regroup_w_in(w_in)
    for i in range(depth):
        g = norm_g[i].reshape(-1, 1, D)
        h = _ffn(h, g[0], wg_all, wu_all, wd_all, (i, 0))

        fb = jnp.zeros((1, F_PAD), F32).at[0, 0:ATT_HEADS].set(f_bias[i])
        h3 = h.reshape(B, S, D)
        qt, k, vt, ccol, crow, y_pool, xs, y_conv = _proj(
            h3, g[1], w_mix_all, i, fb, _block_diag(pool_w[i]).astype(BF16),
            pool_scale[i].reshape(1, W), conv_w[i])
        y_att = _attention(qt, k, vt, crow, ccol)
        bd, lr, li, cd = _ssm_params(ssm_lam_re[i], ssm_lam_im[i], ssm_log_dt[i], ssm_b_re[i],
                                     ssm_b_im[i], ssm_c_re[i], ssm_c_im[i])
        y_ssm = _ssm(xs, bd, lr, li, cd, ssm_d[i].reshape(1, W), ssm_w_glu[i].astype(BF16))
        h3 = _merge(h3, g[1], w_gate_all, i, y_att, y_pool, y_ssm, y_conv,
                    w_branch[i].astype(BF16), w_out[i].astype(BF16))
        h = h3.reshape(T, D)

        h = _ffn(h, g[2], wg_all, wu_all, wd_all, (i, 1),
                 ple_args=(p3, g[3], wpg_all, wpp_all),
                 final_g=final_g.reshape(1, D) if i == depth - 1 else None)
    return h.reshape(B, S, D)
```

```python
import functools

import jax
import jax.numpy as jnp
from jax import lax
from jax.experimental import pallas as pl
from jax.experimental.pallas import tpu as pltpu

F32 = jnp.float32
BF16 = jnp.bfloat16

EPS = 1e-6
NEG_INF = -1e30
LOG2E = 1.4426950408889634

ATT_HEADS = 4
ATT_HEAD_DIM = 64
BRANCH_WIDTH = 256
N_BRANCH = 4
POOL_WINDOWS = (2, 4, 8, 16)
POOL_GROUP_DIM = 64
POOL_HALO = 16
SSM_GROUPS = 16
SSM_GROUP_DIM = 16
SSM_STATE = 64
SSM_CHANNELS = SSM_GROUPS * SSM_STATE
CONV_WIDTH = 3
CONV_HALO = 8
F_PAD = 128
C_ROWS = 8

V7X_SUBLANES = 8
V7X_VMEM_LIMIT_BYTES = 56 * 1024 * 1024

FFN_TILE = 512
PROJ_TILE = 512
MERGE_TILE = 512
ATT_TILE = 512
SSM_STEPS = 64
REGROUP_ROWS = 256


def _sigmoid(x):
    return 0.5 * jnp.tanh(0.5 * x) + 0.5


def _rmsnorm(x, g):
    return x * lax.rsqrt(jnp.mean(x * x, axis=-1, keepdims=True) + EPS) * g


def _dot(a, b):
    return jnp.dot(a, b, preferred_element_type=F32)


def _const_spec(shape):
    zeros = (0,) * len(shape)
    return pl.BlockSpec(shape, lambda *_: zeros, pipeline_mode=pl.Buffered(1))


def _params(*semantics):
    return pltpu.CompilerParams(dimension_semantics=semantics,
                                vmem_limit_bytes=V7X_VMEM_LIMIT_BYTES)


def _ffn_kernel(*refs, ple, final):
    h_ref, g_ref, wg_ref, wu_ref, wd_ref = refs[:5]
    rest = refs[5:]
    x = h_ref[...]
    xn = _rmsnorm(x, g_ref[...]).astype(BF16)
    a = _dot(xn, wg_ref[...])
    b = _dot(xn, wu_ref[...])
    act = (a * _sigmoid(a) * b).astype(BF16)
    y = x + 0.5 * _dot(act, wd_ref[...])
    if ple:
        p_ref, gp_ref, wpg_ref, wpp_ref = rest[:4]
        rest = rest[4:]
        yn = _rmsnorm(y, gp_ref[...]).astype(BF16)
        gate = _sigmoid(_dot(yn, wpg_ref[...]))
        y = y + gate * _dot(p_ref[...].astype(BF16), wpp_ref[...])
    if final:
        gf_ref = rest[0]
        rest = rest[1:]
        y = _rmsnorm(y, gf_ref[...])
    (o_ref,) = rest
    o_ref[...] = y


def _picked_spec(shape, index):
    lead = len(index)
    block = (None,) * lead + tuple(shape[lead:])
    full = tuple(index) + (0,) * (len(shape) - lead)
    return pl.BlockSpec(block, lambda *_: full, pipeline_mode=pl.Buffered(1))


def _ffn(h, g, wg, wu, wd, which, ple_args=None, final_g=None):
    T, D = h.shape
    F = wg.shape[-1]
    tm = min(FFN_TILE, T)
    row = lambda i: (i, 0)
    in_specs = [pl.BlockSpec((tm, D), row), _const_spec((1, D)), _picked_spec(wg.shape, which),
                _picked_spec(wu.shape, which), _picked_spec(wd.shape, which)]
    args = [h, g, wg, wu, wd]
    if ple_args is not None:
        p, gp, wpg, wpp = ple_args
        P = p.shape[-1]
        layer = which[0]
        in_specs += [pl.BlockSpec((None, tm, P), lambda i: (layer, i, 0)), _const_spec((1, D)),
                     _picked_spec(wpg.shape, (layer,)), _picked_spec(wpp.shape, (layer,))]
        args += [p, gp, wpg, wpp]
    if final_g is not None:
        in_specs.append(_const_spec((1, D)))
        args.append(final_g)
    return pl.pallas_call(
        functools.partial(_ffn_kernel, ple=ple_args is not None, final=final_g is not None),
        grid=(T // tm,),
        in_specs=in_specs,
        out_specs=pl.BlockSpec((tm, D), row),
        out_shape=jax.ShapeDtypeStruct((T, D), F32),
        compiler_params=_params("parallel"),
        name="ffn",
    )(*args)


def _proj_kernel(h_ref, g_ref, w_ref, fb_ref, pw_ref, ps_ref, cw_ref,
                 qt_ref, k_ref, vt_ref, ccol_ref, crow_ref, pool_ref, xs_ref, conv_ref,
                 carry_ref, xp_ext, zc_ext, *, tm):
    s_idx = pl.program_id(1)
    W = BRANCH_WIDTH

    @pl.when(s_idx == 0)
    def _():
        carry_ref[...] = jnp.zeros_like(carry_ref)
        xp_ext[0:POOL_HALO, :] = jnp.zeros((POOL_HALO, W), F32)
        zc_ext[0:CONV_HALO, :] = jnp.zeros((CONV_HALO, W), F32)

    u = _rmsnorm(h_ref[...], g_ref[...]).astype(BF16)
    z = _dot(u, w_ref[...])
    qt_ref[...] = (z[:, 0:W] * (LOG2E * ATT_HEAD_DIM ** -0.5)).T.astype(BF16)
    k_ref[...] = z[:, W:2 * W].astype(BF16)
    vt_ref[...] = z[:, 2 * W:3 * W].T.astype(BF16)
    xp = z[:, 3 * W:4 * W]
    xs_ref[...] = z[:, 4 * W:5 * W].astype(BF16)
    cb = z[:, 5 * W:6 * W]
    zc = z[:, 6 * W:7 * W] * z[:, 7 * W:8 * W]
    zf = z[:, 8 * W:8 * W + F_PAD]

    t = (zf + fb_ref[...]).T[0:C_ROWS, :]
    lf = jnp.minimum(t, 0.0) - jnp.log1p(jnp.exp(-jnp.abs(t)))
    hi = lf.astype(BF16).astype(F32)
    mid = (lf - hi).astype(BF16).astype(F32)
    lo = lf - hi - mid
    pieces = jnp.concatenate([hi, mid, lo, jnp.zeros_like(lo)], axis=0).astype(BF16)
    rows = lax.broadcasted_iota(jnp.int32, (tm, tm), 0)
    cols = lax.broadcasted_iota(jnp.int32, (tm, tm), 1)
    upper = jnp.where(rows <= cols, 1.0, 0.0).astype(BF16)
    sums = _dot(pieces, upper)
    c = (sums[0:C_ROWS] + sums[C_ROWS:2 * C_ROWS] + sums[2 * C_ROWS:3 * C_ROWS]
         + carry_ref[:, 0:1])
    carry_ref[...] = jnp.broadcast_to(c[:, tm - 1:tm], carry_ref.shape)
    c2 = c * LOG2E
    crow_ref[...] = c2
    ccol_ref[...] = jnp.concatenate(
        [c2, jnp.zeros((F_PAD - C_ROWS, tm), F32)], axis=0).T

    xp_ext[POOL_HALO:POOL_HALO + tm, :] = xp
    half = W // 2

    def shifted(k, lo_lane):
        return xp_ext[POOL_HALO - k:POOL_HALO - k + tm, lo_lane:lo_lane + half]

    def window(lo_lane, n):
        acc = shifted(0, lo_lane)
        for k in range(1, n):
            acc = acc + shifted(k, lo_lane)
        return acc

    lane = lax.broadcasted_iota(jnp.int32, (tm, half), 1)
    pos = s_idx * tm + lax.broadcasted_iota(jnp.int32, (tm, half), 0)
    first = lane < POOL_GROUP_DIM
    halves = []
    for hidx in range(2):
        w_small, w_big = POOL_WINDOWS[2 * hidx], POOL_WINDOWS[2 * hidx + 1]
        lo_lane = hidx * half
        sum_small = window(lo_lane, w_small)
        sum_big = sum_small
        for k in range(w_small, w_big):
            sum_big = sum_big + shifted(k, lo_lane)
        wsum = jnp.where(first, sum_small, sum_big)
        win = jnp.where(first, w_small, w_big)
        count = jnp.minimum(pos + 1, win).astype(F32)
        halves.append(wsum / count - shifted(0, lo_lane))
    pooled = jnp.concatenate(halves, axis=1).astype(BF16)
    pool_ref[...] = (_dot(pooled, pw_ref[...]) * ps_ref[...]).astype(BF16)
    xp_ext[0:POOL_HALO, :] = xp_ext[tm:tm + POOL_HALO, :]

    zc_ext[CONV_HALO:CONV_HALO + tm, :] = zc
    y = cw_ref[CONV_WIDTH - 1:CONV_WIDTH, :] * zc
    for j in range(CONV_WIDTH - 1):
        back = CONV_WIDTH - 1 - j
        y = y + cw_ref[j:j + 1, :] * zc_ext[CONV_HALO - back:CONV_HALO - back + tm, :]
    conv_ref[...] = (cb * y).astype(BF16)
    zc_ext[0:CONV_HALO, :] = zc_ext[tm:tm + CONV_HALO, :]


def _proj(h3, g, w, layer, fb, pw, ps, cw):
    B, S, D = h3.shape
    W = BRANCH_WIDTH
    tm = min(PROJ_TILE, S)
    tok = lambda b, s: (b, s, 0)
    act = jax.ShapeDtypeStruct((B, S, W), BF16)
    act_spec = pl.BlockSpec((None, tm, W), tok)
    act_t = jax.ShapeDtypeStruct((B, W, S), BF16)
    act_t_spec = pl.BlockSpec((None, W, tm), lambda b, s: (b, 0, s))
    return pl.pallas_call(
        functools.partial(_proj_kernel, tm=tm),
        grid=(B, S // tm),
        in_specs=[pl.BlockSpec((None, tm, D), tok), _const_spec((1, D)),
                  _picked_spec(w.shape, (layer,)), _const_spec((1, F_PAD)), _const_spec((W, W)),
                  _const_spec((1, W)), _const_spec((CONV_WIDTH, W))],
        out_specs=[act_t_spec, act_spec, act_t_spec,
                   pl.BlockSpec((None, tm, F_PAD), tok),
                   pl.BlockSpec((None, C_ROWS, tm), lambda b, s: (b, 0, s)),
                   act_spec, act_spec, act_spec],
        out_shape=[act_t, act, act_t,
                   jax.ShapeDtypeStruct((B, S, F_PAD), F32),
                   jax.ShapeDtypeStruct((B, C_ROWS, S), F32),
                   act, act, act],
        scratch_shapes=[pltpu.VMEM((C_ROWS, F_PAD), F32),
                        pltpu.VMEM((POOL_HALO + tm, W), F32),
                        pltpu.VMEM((CONV_HALO + tm, W), F32)],
        compiler_params=_params("arbitrary", "arbitrary"),
        name="proj",
    )(h3, g, w, fb, pw, ps, cw)


def _att_kernel(qt_ref, k_ref, vt_ref, crow_ref, ccol_ref, o_ref, *, tile):
    qi = pl.program_id(1)
    W = BRANCH_WIDTH
    Dh = ATT_HEAD_DIM
    qt = qt_ref[...]
    feat = lax.broadcasted_iota(jnp.int32, (W, tile), 0)
    qts = [jnp.where((feat >= h * Dh) & (feat < (h + 1) * Dh), qt, jnp.zeros_like(qt))
           for h in range(ATT_HEADS)]
    c_t = [crow_ref[h:h + 1, :] for h in range(ATT_HEADS)]
    kv_pos = lax.broadcasted_iota(jnp.int32, (tile, tile), 0)
    q_pos = lax.broadcasted_iota(jnp.int32, (tile, tile), 1)

    def k_dot_q(j):
        start = pl.multiple_of(j * tile, tile)
        kj = k_ref[pl.ds(start, tile), :]
        return tuple(_dot(kj, qts[h]) for h in range(ATT_HEADS))

    def block(j, stats, diagonal):
        start = pl.multiple_of(j * tile, tile)
        kq = k_dot_q(j)
        new = []
        for h in range(ATT_HEADS):
            m, l, acc = stats[h]
            t = kq[h] - ccol_ref[pl.ds(start, tile), h:h + 1]
            if diagonal:
                t = jnp.where(kv_pos <= q_pos, t, NEG_INF)
            m_new = jnp.maximum(m, jnp.max(t, axis=0, keepdims=True) + c_t[h])
            alpha = jnp.exp2(m - m_new)
            p = jnp.exp2(t + (c_t[h] - m_new))
            l = alpha * l + jnp.sum(p, axis=0, keepdims=True)
            vth = vt_ref[h * Dh:(h + 1) * Dh, pl.ds(start, tile)]
            acc = alpha * acc + _dot(vth, p.astype(BF16))
            new.append((m_new, l, acc))
        return tuple(new)

    init = tuple((jnp.full((1, tile), NEG_INF, F32), jnp.zeros((1, tile), F32),
                  jnp.zeros((Dh, tile), F32)) for _ in range(ATT_HEADS))
    stats = lax.fori_loop(0, qi, lambda j, c: block(j, c, False), init)
    stats = block(qi, stats, True)
    out_t = jnp.concatenate([acc / l for (_, l, acc) in stats], axis=0)
    o_ref[...] = out_t.T.astype(o_ref.dtype)


def _attention(qt, k, vt, crow, ccol):
    B, S, W = k.shape
    tile = min(ATT_TILE, S)
    seq = lambda b, i: (b, 0, 0)
    qcol = lambda b, i: (b, 0, i)
    return pl.pallas_call(
        functools.partial(_att_kernel, tile=tile),
        grid=(B, S // tile),
        in_specs=[pl.BlockSpec((None, W, tile), qcol),
                  pl.BlockSpec((None, S, W), seq),
                  pl.BlockSpec((None, W, S), seq),
                  pl.BlockSpec((None, C_ROWS, tile), qcol),
                  pl.BlockSpec((None, S, F_PAD), seq)],
        out_specs=pl.BlockSpec((None, tile, W), lambda b, i: (b, i, 0)),
        out_shape=jax.ShapeDtypeStruct((B, S, W), BF16),
        compiler_params=_params("parallel", "arbitrary"),
        name="attention",
    )(qt, k, vt, crow, ccol)


def _dot_row_halves(lhs_ref, rhs, cast=None):
    half = lhs_ref.shape[0] // 2
    parts = []
    for r in range(2):
        lhs = lhs_ref[r * half:(r + 1) * half, :]
        parts.append(_dot(lhs if cast is None else lhs.astype(cast), rhs))
    return jnp.concatenate(parts, axis=0)


def _ssm_kernel(u_ref, perm_ref, perm_t_ref, bd_ref, lr_ref, li_ref, cd_ref, d_ref, wglu_ref,
                o_ref, state_ref, x_ref, *, steps, batch):
    N = SSM_CHANNELS
    W = BRANCH_WIDTH
    rows = steps * batch

    @pl.when(pl.program_id(0) == 0)
    def _():
        state_ref[...] = jnp.zeros_like(state_ref)

    u = _dot_row_halves(perm_ref, u_ref[...].reshape(rows, W)).astype(BF16)
    x_ref[...] = _dot(u, bd_ref[...])
    lr = jnp.broadcast_to(lr_ref[...], (batch, N))
    li = jnp.broadcast_to(li_ref[...], (batch, N))

    def step(t, carry):
        sr, si = carry
        r0 = pl.multiple_of(t * batch, batch)
        nr = lr * sr - li * si + x_ref[pl.ds(r0, batch), 0:N]
        ni = lr * si + li * sr + x_ref[pl.ds(r0, batch), N:2 * N]
        x_ref[pl.ds(r0, batch), 0:N] = nr
        x_ref[pl.ds(r0, batch), N:2 * N] = ni
        return nr, ni

    sr, si = lax.fori_loop(0, steps, step, (state_ref[:, 0:N], state_ref[:, N:2 * N]),
                           unroll=True)
    state_ref[:, 0:N] = sr
    state_ref[:, N:2 * N] = si

    y = _dot_row_halves(x_ref, cd_ref[...], cast=BF16) + d_ref[...] * u.astype(F32)
    yg = _dot(y.astype(BF16), wglu_ref[...])
    out = (yg[:, 0:W] * _sigmoid(yg[:, W:2 * W])).astype(BF16)
    o_ref[...] = _dot_row_halves(perm_t_ref, out).astype(o_ref.dtype).reshape(batch, steps, W)


def _ssm(xs, bd, lr, li, cd, d, wglu):
    B, S, W = xs.shape
    steps = min(SSM_STEPS, S)
    rows = steps * B
    N = SSM_CHANNELS
    src = (jnp.arange(rows) % B) * steps + jnp.arange(rows) // B
    perm = (src[:, None] == jnp.arange(rows)[None, :]).astype(BF16)
    chunk = lambda i: (0, i, 0)
    return pl.pallas_call(
        functools.partial(_ssm_kernel, steps=steps, batch=B),
        grid=(S // steps,),
        in_specs=[pl.BlockSpec((B, steps, W), chunk), _const_spec((rows, rows)),
                  _const_spec((rows, rows)), _const_spec((W, 2 * N)),
                  _const_spec((1, N)), _const_spec((1, N)), _const_spec((2 * N, W)),
                  _const_spec((1, W)), _const_spec((W, 2 * W))],
        out_specs=pl.BlockSpec((B, steps, W), chunk),
        out_shape=jax.ShapeDtypeStruct((B, S, W), BF16),
        scratch_shapes=[pltpu.VMEM((B, 2 * N), F32), pltpu.VMEM((rows, 2 * N), F32)],
        compiler_params=_params("arbitrary"),
        name="ssm",
    )(xs, perm, perm.T, bd, lr, li, cd, d, wglu)


def _merge_kernel(h_ref, g_ref, wgate_ref, att_ref, pool_ref, ssm_ref, conv_ref,
                  wbr_ref, wout_ref, o_ref):
    D = h_ref.shape[-1]
    x = h_ref[...]
    u = _rmsnorm(x, g_ref[...]).astype(BF16)
    merged = None
    for n, y_ref in enumerate((att_ref, pool_ref, ssm_ref, conv_ref)):
        gate = _sigmoid(_dot(u, wgate_ref[:, n * D:(n + 1) * D]))
        term = gate * _dot(y_ref[...], wbr_ref[n])
        merged = term if merged is None else merged + term
    o_ref[...] = x + _dot(merged.astype(BF16), wout_ref[...])


def _merge(h3, g, wgate, layer, y_att, y_pool, y_ssm, y_conv, wbr, wout):
    B, S, D = h3.shape
    W = BRANCH_WIDTH
    tm = min(MERGE_TILE, S)
    tok = lambda b, s: (b, s, 0)
    act_spec = pl.BlockSpec((None, tm, W), tok)
    return pl.pallas_call(
        _merge_kernel,
        grid=(B, S // tm),
        in_specs=[pl.BlockSpec((None, tm, D), tok), _const_spec((1, D)),
                  _picked_spec(wgate.shape, (layer,)), act_spec, act_spec, act_spec, act_spec,
                  _const_spec((N_BRANCH, W, D)), _const_spec((D, D))],
        out_specs=pl.BlockSpec((None, tm, D), tok),
        out_shape=jax.ShapeDtypeStruct((B, S, D), F32),
        compiler_params=_params("parallel", "parallel"),
        name="merge",
    )(h3, g, wgate, y_att, y_pool, y_ssm, y_conv, wbr, wout)


def _regroup_kernel(w_ref, mix_ref, gate_ref):
    W = BRANCH_WIDTH
    n_mix = 3 * W + ATT_HEADS
    mix_cols = 8 * W + ATT_HEADS
    w = w_ref[...]
    rows = w.shape[0]
    mix_ref[:, 0:3 * W] = w[:, 0:3 * W].astype(BF16)
    mix_ref[:, 3 * W:8 * W] = w[:, n_mix:mix_cols].astype(BF16)
    mix_ref[:, 8 * W:8 * W + F_PAD] = jnp.concatenate(
        [w[:, 3 * W:n_mix], jnp.zeros((rows, F_PAD - ATT_HEADS), F32)], axis=1).astype(BF16)
    gate_ref[...] = w[:, mix_cols:].astype(BF16)


def _regroup_w_in(w_in):
    depth, D, cols = w_in.shape
    W = BRANCH_WIDTH
    n_gate = cols - (8 * W + ATT_HEADS)
    rows = min(REGROUP_ROWS, D)
    blk = lambda i, r: (i, r, 0)
    return pl.pallas_call(
        _regroup_kernel,
        grid=(depth, D // rows),
        in_specs=[pl.BlockSpec((None, rows, cols), blk)],
        out_specs=[pl.BlockSpec((None, rows, 8 * W + F_PAD), blk),
                   pl.BlockSpec((None, rows, n_gate), blk)],
        out_shape=[jax.ShapeDtypeStruct((depth, D, 8 * W + F_PAD), BF16),
                   jax.ShapeDtypeStruct((depth, D, n_gate), BF16)],
        compiler_params=_params("parallel", "parallel"),
        name="regroup_w_in",
    )(w_in)


def _block_diag(blocks):
    G, a, b = blocks.shape
    eye = jnp.eye(G, dtype=blocks.dtype)
    return (eye[:, None, :, None] * blocks[:, :, None, :]).reshape(G * a, G * b)


def _ssm_params(lam_re, lam_im, log_dt, b_re, b_im, c_re, c_im):
    dt = jnp.exp(log_dt)[:, None]
    lr = jnp.minimum(lam_re, -1e-4)
    li = lam_im
    mag = jnp.exp(lr * dt)
    lbr = mag * jnp.cos(li * dt)
    lbi = mag * jnp.sin(li * dt)
    den = lr * lr + li * li
    cr = ((lbr - 1.0) * lr + lbi * li) / den
    ci = (lbi * lr - (lbr - 1.0) * li) / den
    bbr = cr[..., None] * b_re - ci[..., None] * b_im
    bbi = cr[..., None] * b_im + ci[..., None] * b_re
    bd = jnp.concatenate([_block_diag(bbr.transpose(0, 2, 1)),
                          _block_diag(bbi.transpose(0, 2, 1))], axis=1)
    cd = jnp.concatenate([_block_diag(c_re.transpose(0, 2, 1)),
                          _block_diag(-c_im.transpose(0, 2, 1))], axis=0)
    return bd.astype(BF16), lbr.reshape(1, -1), lbi.reshape(1, -1), cd.astype(BF16)


def kernel(x, p, norm_g, ffn_w_gate, ffn_w_up, ffn_w_down, w_in, f_bias, pool_w, pool_scale,
           ssm_lam_re, ssm_lam_im, ssm_log_dt, ssm_b_re, ssm_b_im, ssm_c_re, ssm_c_im, ssm_d,
           ssm_w_glu, conv_w, w_branch, w_out, ple_w_gate, ple_w_proj, final_g):
    B, S, D = x.shape
    depth = norm_g.shape[0]
    T = B * S
    W = BRANCH_WIDTH
    assert B == V7X_SUBLANES, "the SSM keeps one batch row per sublane"
    h = x.reshape(T, D)
    wg_all, wu_all, wd_all = (w.astype(BF16) for w in (ffn_w_gate, ffn_w_up, ffn_w_down))
    wpg_all, wpp_all = ple_w_gate.astype(BF16), ple_w_proj.astype(BF16)
    p3 = p.reshape(depth, T, p.shape[-1])
    w_mix_all, w_gate_all = _regroup_w_in(w_in)
    for i in range(depth):
        g = norm_g[i].reshape(-1, 1, D)
        h = _ffn(h, g[0], wg_all, wu_all, wd_all, (i, 0))

        fb = jnp.zeros((1, F_PAD), F32).at[0, 0:ATT_HEADS].set(f_bias[i])
        h3 = h.reshape(B, S, D)
        qt, k, vt, ccol, crow, y_pool, xs, y_conv = _proj(
            h3, g[1], w_mix_all, i, fb, _block_diag(pool_w[i]).astype(BF16),
            pool_scale[i].reshape(1, W), conv_w[i])
        y_att = _attention(qt, k, vt, crow, ccol)
        bd, lr, li, cd = _ssm_params(ssm_lam_re[i], ssm_lam_im[i], ssm_log_dt[i], ssm_b_re[i],
                                     ssm_b_im[i], ssm_c_re[i], ssm_c_im[i])
        y_ssm = _ssm(xs, bd, lr, li, cd, ssm_d[i].reshape(1, W), ssm_w_glu[i].astype(BF16))
        h3 = _merge(h3, g[1], w_gate_all, i, y_att, y_pool, y_ssm, y_conv,
                    w_branch[i].astype(BF16), w_out[i].astype(BF16))
        h = h3.reshape(T, D)

        h = _ffn(h, g[2], wg_all, wu_all, wd_all, (i, 1),
                 ple_args=(p3, g[3], wpg_all, wpp_all),
                 final_g=final_g.reshape(1, D) if i == depth - 1 else None)
    return h.reshape(B, S, D)
```

```python
import functools

import jax
import jax.numpy as jnp
from jax import lax
from jax.experimental import pallas as pl
from jax.experimental.pallas import tpu as pltpu

F32 = jnp.float32
BF16 = jnp.bfloat16

EPS = 1e-6
NEG_INF = -1e30
LOG2E = 1.4426950408889634

ATT_HEADS = 4
ATT_HEAD_DIM = 64
BRANCH_WIDTH = 256
N_BRANCH = 4
POOL_WINDOWS = (2, 4, 8, 16)
POOL_GROUP_DIM = 64
POOL_HALO = 16
SSM_GROUPS = 16
SSM_GROUP_DIM = 16
SSM_STATE = 64
SSM_CHANNELS = SSM_GROUPS * SSM_STATE
CONV_WIDTH = 3
CONV_HALO = 8
F_PAD = 128
C_ROWS = 8
ONES_ROWS = 16

V7X_SUBLANES = 8
V7X_VMEM_LIMIT_BYTES = 56 * 1024 * 1024

FFN_TILE = 512
PROJ_TILE = 512
MERGE_TILE = 512
ATT_TILE = 512
SSM_STEPS = 64
REGROUP_ROWS = 256


def _sigmoid(x):
    return 0.5 * jnp.tanh(0.5 * x) + 0.5


def _rmsnorm(x, g):
    return x * lax.rsqrt(jnp.mean(x * x, axis=-1, keepdims=True) + EPS) * g


def _dot(a, b):
    return jnp.dot(a, b, preferred_element_type=F32)


def _const_spec(shape):
    zeros = (0,) * len(shape)
    return pl.BlockSpec(shape, lambda *_: zeros, pipeline_mode=pl.Buffered(1))


def _params(*semantics):
    return pltpu.CompilerParams(dimension_semantics=semantics,
                                vmem_limit_bytes=V7X_VMEM_LIMIT_BYTES)


def _ffn_kernel(*refs, ple, final):
    h_ref, g_ref, wg_ref, wu_ref, wd_ref = refs[:5]
    rest = refs[5:]
    x = h_ref[...]
    xn = _rmsnorm(x, g_ref[...]).astype(BF16)
    a = _dot(xn, wg_ref[...])
    b = _dot(xn, wu_ref[...])
    act = (a * _sigmoid(a) * b).astype(BF16)
    y = x + 0.5 * _dot(act, wd_ref[...])
    if ple:
        p_ref, gp_ref, wpg_ref, wpp_ref = rest[:4]
        rest = rest[4:]
        yn = _rmsnorm(y, gp_ref[...]).astype(BF16)
        gate = _sigmoid(_dot(yn, wpg_ref[...]))
        y = y + gate * _dot(p_ref[...].astype(BF16), wpp_ref[...])
    if final:
        gf_ref = rest[0]
        rest = rest[1:]
        y = _rmsnorm(y, gf_ref[...])
    (o_ref,) = rest
    o_ref[...] = y


def _picked_spec(shape, index):
    lead = len(index)
    block = (None,) * lead + tuple(shape[lead:])
    full = tuple(index) + (0,) * (len(shape) - lead)
    return pl.BlockSpec(block, lambda *_: full, pipeline_mode=pl.Buffered(1))


def _ffn(h, g, wg, wu, wd, which, ple_args=None, final_g=None):
    T, D = h.shape
    F = wg.shape[-1]
    tm = min(FFN_TILE, T)
    row = lambda i: (i, 0)
    in_specs = [pl.BlockSpec((tm, D), row), _const_spec((1, D)), _picked_spec(wg.shape, which),
                _picked_spec(wu.shape, which), _picked_spec(wd.shape, which)]
    args = [h, g, wg, wu, wd]
    if ple_args is not None:
        p, gp, wpg, wpp = ple_args
        P = p.shape[-1]
        layer = which[0]
        in_specs += [pl.BlockSpec((None, tm, P), lambda i: (layer, i, 0)), _const_spec((1, D)),
                     _picked_spec(wpg.shape, (layer,)), _picked_spec(wpp.shape, (layer,))]
        args += [p, gp, wpg, wpp]
    if final_g is not None:
        in_specs.append(_const_spec((1, D)))
        args.append(final_g)
    return pl.pallas_call(
        functools.partial(_ffn_kernel, ple=ple_args is not None, final=final_g is not None),
        grid=(T // tm,),
        in_specs=in_specs,
        out_specs=pl.BlockSpec((tm, D), row),
        out_shape=jax.ShapeDtypeStruct((T, D), F32),
        compiler_params=_params("parallel"),
        name="ffn",
    )(*args)


def _proj_kernel(h_ref, g_ref, w_ref, fb_ref, pw_ref, ps_ref, cw_ref,
                 qt_ref, k_ref, vt_ref, ccol_ref, crow_ref, pool_ref, xs_ref, conv_ref,
                 carry_ref, xp_ext, zc_ext, *, tm):
    s_idx = pl.program_id(1)
    W = BRANCH_WIDTH

    @pl.when(s_idx == 0)
    def _():
        carry_ref[...] = jnp.zeros_like(carry_ref)
        xp_ext[0:POOL_HALO, :] = jnp.zeros((POOL_HALO, W), F32)
        zc_ext[0:CONV_HALO, :] = jnp.zeros((CONV_HALO, W), F32)

    u = _rmsnorm(h_ref[...], g_ref[...]).astype(BF16)
    z = _dot(u, w_ref[...])
    qt_ref[...] = (z[:, 0:W] * (LOG2E * ATT_HEAD_DIM ** -0.5)).T.astype(BF16)
    k_ref[...] = z[:, W:2 * W].astype(BF16)
    vt_ref[...] = z[:, 2 * W:3 * W].T.astype(BF16)
    xp = z[:, 3 * W:4 * W]
    xs_ref[...] = z[:, 4 * W:5 * W].astype(BF16)
    cb = z[:, 5 * W:6 * W]
    zc = z[:, 6 * W:7 * W] * z[:, 7 * W:8 * W]
    zf = z[:, 8 * W:8 * W + F_PAD]

    t = (zf + fb_ref[...]).T[0:C_ROWS, :]
    lf = jnp.minimum(t, 0.0) - jnp.log1p(jnp.exp(-jnp.abs(t)))
    hi = lf.astype(BF16).astype(F32)
    mid = (lf - hi).astype(BF16).astype(F32)
    lo = lf - hi - mid
    pieces = jnp.concatenate([hi, mid, lo, jnp.zeros_like(lo)], axis=0).astype(BF16)
    rows = lax.broadcasted_iota(jnp.int32, (tm, tm), 0)
    cols = lax.broadcasted_iota(jnp.int32, (tm, tm), 1)
    upper = jnp.where(rows <= cols, 1.0, 0.0).astype(BF16)
    sums = _dot(pieces, upper)
    c = (sums[0:C_ROWS] + sums[C_ROWS:2 * C_ROWS] + sums[2 * C_ROWS:3 * C_ROWS]
         + carry_ref[:, 0:1])
    carry_ref[...] = jnp.broadcast_to(c[:, tm - 1:tm], carry_ref.shape)
    c2 = c * LOG2E
    crow_ref[...] = c2
    ccol_ref[...] = jnp.concatenate(
        [c2, jnp.zeros((F_PAD - C_ROWS, tm), F32)], axis=0).T

    xp_ext[POOL_HALO:POOL_HALO + tm, :] = xp
    half = W // 2

    def shifted(k, lo_lane):
        return xp_ext[POOL_HALO - k:POOL_HALO - k + tm, lo_lane:lo_lane + half]

    def window(lo_lane, n):
        acc = shifted(0, lo_lane)
        for k in range(1, n):
            acc = acc + shifted(k, lo_lane)
        return acc

    lane = lax.broadcasted_iota(jnp.int32, (tm, half), 1)
    pos = s_idx * tm + lax.broadcasted_iota(jnp.int32, (tm, half), 0)
    first = lane < POOL_GROUP_DIM
    halves = []
    for hidx in range(2):
        w_small, w_big = POOL_WINDOWS[2 * hidx], POOL_WINDOWS[2 * hidx + 1]
        lo_lane = hidx * half
        sum_small = window(lo_lane, w_small)
        sum_big = sum_small
        for k in range(w_small, w_big):
            sum_big = sum_big + shifted(k, lo_lane)
        wsum = jnp.where(first, sum_small, sum_big)
        win = jnp.where(first, w_small, w_big)
        count = jnp.minimum(pos + 1, win).astype(F32)
        halves.append(wsum / count - shifted(0, lo_lane))
    pooled = jnp.concatenate(halves, axis=1).astype(BF16)
    pool_ref[...] = (_dot(pooled, pw_ref[...]) * ps_ref[...]).astype(BF16)
    xp_ext[0:POOL_HALO, :] = xp_ext[tm:tm + POOL_HALO, :]

    zc_ext[CONV_HALO:CONV_HALO + tm, :] = zc
    y = cw_ref[CONV_WIDTH - 1:CONV_WIDTH, :] * zc
    for j in range(CONV_WIDTH - 1):
        back = CONV_WIDTH - 1 - j
        y = y + cw_ref[j:j + 1, :] * zc_ext[CONV_HALO - back:CONV_HALO - back + tm, :]
    conv_ref[...] = (cb * y).astype(BF16)
    zc_ext[0:CONV_HALO, :] = zc_ext[tm:tm + CONV_HALO, :]


def _proj(h3, g, w, layer, fb, pw, ps, cw):
    B, S, D = h3.shape
    W = BRANCH_WIDTH
    tm = min(PROJ_TILE, S)
    tok = lambda b, s: (b, s, 0)
    act = jax.ShapeDtypeStruct((B, S, W), BF16)
    act_spec = pl.BlockSpec((None, tm, W), tok)
    act_t = jax.ShapeDtypeStruct((B, W, S), BF16)
    act_t_spec = pl.BlockSpec((None, W, tm), lambda b, s: (b, 0, s))
    return pl.pallas_call(
        functools.partial(_proj_kernel, tm=tm),
        grid=(B, S // tm),
        in_specs=[pl.BlockSpec((None, tm, D), tok), _const_spec((1, D)),
                  _picked_spec(w.shape, (layer,)), _const_spec((1, F_PAD)), _const_spec((W, W)),
                  _const_spec((1, W)), _const_spec((CONV_WIDTH, W))],
        out_specs=[act_t_spec, act_spec, act_t_spec,
                   pl.BlockSpec((None, tm, F_PAD), tok),
                   pl.BlockSpec((None, C_ROWS, tm), lambda b, s: (b, 0, s)),
                   act_spec, act_spec, act_spec],
        out_shape=[act_t, act, act_t,
                   jax.ShapeDtypeStruct((B, S, F_PAD), F32),
                   jax.ShapeDtypeStruct((B, C_ROWS, S), F32),
                   act, act, act],
        scratch_shapes=[pltpu.VMEM((C_ROWS, F_PAD), F32),
                        pltpu.VMEM((POOL_HALO + tm, W), F32),
                        pltpu.VMEM((CONV_HALO + tm, W), F32)],
        compiler_params=_params("arbitrary", "arbitrary"),
        name="proj",
    )(h3, g, w, fb, pw, ps, cw)


def _att_kernel(qt_ref, k_ref, vt_ref, crow_ref, ccol_ref, o_ref, *, tile):
    qi = pl.program_id(1)
    W = BRANCH_WIDTH
    Dh = ATT_HEAD_DIM
    qt = qt_ref[...]
    feat = lax.broadcasted_iota(jnp.int32, (W, tile), 0)
    qts = [jnp.where((feat >= h * Dh) & (feat < (h + 1) * Dh), qt, jnp.zeros_like(qt))
           for h in range(ATT_HEADS)]
    c_t = [crow_ref[h:h + 1, :] for h in range(ATT_HEADS)]
    kv_pos = lax.broadcasted_iota(jnp.int32, (tile, tile), 0)
    q_pos = lax.broadcasted_iota(jnp.int32, (tile, tile), 1)

    def block(j, stats, diagonal):
        start = pl.multiple_of(j * tile, tile)
        kj = k_ref[pl.ds(start, tile), :]
        ts = [_dot(kj, qts[h]) - ccol_ref[pl.ds(start, tile), h:h + 1]
              for h in range(ATT_HEADS)]
        new = []
        for h in range(ATT_HEADS):
            m, acc = stats[h]
            t = ts[h]
            if diagonal:
                t = jnp.where(kv_pos <= q_pos, t, NEG_INF)
            m_new = jnp.maximum(m, jnp.max(t, axis=0, keepdims=True) + c_t[h])
            alpha = jnp.exp2(m - m_new)
            p = jnp.exp2(t + (c_t[h] - m_new)).astype(BF16)
            vth = jnp.concatenate([vt_ref[h * Dh:(h + 1) * Dh, pl.ds(start, tile)], ones],
                                  axis=0)
            acc = alpha * acc + _dot(vth, p)
            new.append((m_new, acc))
        return tuple(new)

    ones = jnp.ones((ONES_ROWS, tile), BF16)
    init = tuple((jnp.full((1, tile), NEG_INF, F32), jnp.zeros((Dh + ONES_ROWS, tile), F32))
                 for _ in range(ATT_HEADS))
    stats = lax.fori_loop(0, qi, lambda j, c: block(j, c, False), init)
    stats = block(qi, stats, True)
    out_t = jnp.concatenate([acc[0:Dh] / acc[Dh:Dh + 1] for (_, acc) in stats],
                            axis=0)
    o_ref[...] = out_t.T.astype(o_ref.dtype)


def _attention(qt, k, vt, crow, ccol):
    B, S, W = k.shape
    tile = min(ATT_TILE, S)
    seq = lambda b, i: (b, 0, 0)
    qcol = lambda b, i: (b, 0, i)
    return pl.pallas_call(
        functools.partial(_att_kernel, tile=tile),
        grid=(B, S // tile),
        in_specs=[pl.BlockSpec((None, W, tile), qcol),
                  pl.BlockSpec((None, S, W), seq),
                  pl.BlockSpec((None, W, S), seq),
                  pl.BlockSpec((None, C_ROWS, tile), qcol),
                  pl.BlockSpec((None, S, F_PAD), seq)],
        out_specs=pl.BlockSpec((None, tile, W), lambda b, i: (b, i, 0)),
        out_shape=jax.ShapeDtypeStruct((B, S, W), BF16),
        compiler_params=_params("parallel", "arbitrary"),
        name="attention",
    )(qt, k, vt, crow, ccol)


def _dot_row_halves(lhs_ref, rhs, cast=None):
    half = lhs_ref.shape[0] // 2
    parts = []
    for r in range(2):
        lhs = lhs_ref[r * half:(r + 1) * half, :]
        parts.append(_dot(lhs if cast is None else lhs.astype(cast), rhs))
    return jnp.concatenate(parts, axis=0)


def _ssm_kernel(u_ref, perm_ref, perm_t_ref, bd_ref, lr_ref, li_ref, cd_ref, d_ref, wglu_ref,
                o_ref, state_ref, x_ref, *, steps, batch):
    N = SSM_CHANNELS
    W = BRANCH_WIDTH
    rows = steps * batch

    @pl.when(pl.program_id(0) == 0)
    def _():
        state_ref[...] = jnp.zeros_like(state_ref)

    u = _dot_row_halves(perm_ref, u_ref[...].reshape(rows, W)).astype(BF16)
    x_ref[...] = _dot(u, bd_ref[...])
    lr = jnp.broadcast_to(lr_ref[...], (batch, N))
    li = jnp.broadcast_to(li_ref[...], (batch, N))

    def step(t, carry):
        sr, si = carry
        r0 = pl.multiple_of(t * batch, batch)
        nr = lr * sr - li * si + x_ref[pl.ds(r0, batch), 0:N]
        ni = lr * si + li * sr + x_ref[pl.ds(r0, batch), N:2 * N]
        x_ref[pl.ds(r0, batch), 0:N] = nr
        x_ref[pl.ds(r0, batch), N:2 * N] = ni
        return nr, ni

    sr, si = lax.fori_loop(0, steps, step, (state_ref[:, 0:N], state_ref[:, N:2 * N]),
                           unroll=True)
    state_ref[:, 0:N] = sr
    state_ref[:, N:2 * N] = si

    y = _dot_row_halves(x_ref, cd_ref[...], cast=BF16) + d_ref[...] * u.astype(F32)
    yg = _dot(y.astype(BF16), wglu_ref[...])
    out = (yg[:, 0:W] * _sigmoid(yg[:, W:2 * W])).astype(BF16)
    o_ref[...] = _dot_row_halves(perm_t_ref, out).astype(o_ref.dtype).reshape(batch, steps, W)


def _ssm(xs, bd, lr, li, cd, d, wglu):
    B, S, W = xs.shape
    steps = min(SSM_STEPS, S)
    rows = steps * B
    N = SSM_CHANNELS
    src = (jnp.arange(rows) % B) * steps + jnp.arange(rows) // B
    perm = (src[:, None] == jnp.arange(rows)[None, :]).astype(BF16)
    chunk = lambda i: (0, i, 0)
    return pl.pallas_call(
        functools.partial(_ssm_kernel, steps=steps, batch=B),
        grid=(S // steps,),
        in_specs=[pl.BlockSpec((B, steps, W), chunk), _const_spec((rows, rows)),
                  _const_spec((rows, rows)), _const_spec((W, 2 * N)),
                  _const_spec((1, N)), _const_spec((1, N)), _const_spec((2 * N, W)),
                  _const_spec((1, W)), _const_spec((W, 2 * W))],
        out_specs=pl.BlockSpec((B, steps, W), chunk),
        out_shape=jax.ShapeDtypeStruct((B, S, W), BF16),
        scratch_shapes=[pltpu.VMEM((B, 2 * N), F32), pltpu.VMEM((rows, 2 * N), F32)],
        compiler_params=_params("arbitrary"),
        name="ssm",
    )(xs, perm, perm.T, bd, lr, li, cd, d, wglu)


def _merge_kernel(h_ref, g_ref, wgate_ref, att_ref, pool_ref, ssm_ref, conv_ref,
                  wbr_ref, wout_ref, o_ref):
    D = h_ref.shape[-1]
    x = h_ref[...]
    u = _rmsnorm(x, g_ref[...]).astype(BF16)
    merged = None
    for n, y_ref in enumerate((att_ref, pool_ref, ssm_ref, conv_ref)):
        gate = _sigmoid(_dot(u, wgate_ref[:, n * D:(n + 1) * D]))
        term = gate * _dot(y_ref[...], wbr_ref[n])
        merged = term if merged is None else merged + term
    o_ref[...] = x + _dot(merged.astype(BF16), wout_ref[...])


def _merge(h3, g, wgate, layer, y_att, y_pool, y_ssm, y_conv, wbr, wout):
    B, S, D = h3.shape
    W = BRANCH_WIDTH
    tm = min(MERGE_TILE, S)
    tok = lambda b, s: (b, s, 0)
    act_spec = pl.BlockSpec((None, tm, W), tok)
    return pl.pallas_call(
        _merge_kernel,
        grid=(B, S // tm),
        in_specs=[pl.BlockSpec((None, tm, D), tok), _const_spec((1, D)),
                  _picked_spec(wgate.shape, (layer,)), act_spec, act_spec, act_spec, act_spec,
                  _const_spec((N_BRANCH, W, D)), _const_spec((D, D))],
        out_specs=pl.BlockSpec((None, tm, D), tok),
        out_shape=jax.ShapeDtypeStruct((B, S, D), F32),
        compiler_params=_params("parallel", "parallel"),
        name="merge",
    )(h3, g, wgate, y_att, y_pool, y_ssm, y_conv, wbr, wout)


def _regroup_kernel(w_ref, mix_ref, gate_ref):
    W = BRANCH_WIDTH
    n_mix = 3 * W + ATT_HEADS
    mix_cols = 8 * W + ATT_HEADS
    w = w_ref[...]
    rows = w.shape[0]
    mix_ref[:, 0:3 * W] = w[:, 0:3 * W].astype(BF16)
    mix_ref[:, 3 * W:8 * W] = w[:, n_mix:mix_cols].astype(BF16)
    mix_ref[:, 8 * W:8 * W + F_PAD] = jnp.concatenate(
        [w[:, 3 * W:n_mix], jnp.zeros((rows, F_PAD - ATT_HEADS), F32)], axis=1).astype(BF16)
    gate_ref[...] = w[:, mix_cols:].astype(BF16)


def _regroup_w_in(w_in):
    depth, D, cols = w_in.shape
    W = BRANCH_WIDTH
    n_gate = cols - (8 * W + ATT_HEADS)
    rows = min(REGROUP_ROWS, D)
    blk = lambda i, r: (i, r, 0)
    return pl.pallas_call(
        _regroup_kernel,
        grid=(depth, D // rows),
        in_specs=[pl.BlockSpec((None, rows, cols), blk)],
        out_specs=[pl.BlockSpec((None, rows, 8 * W + F_PAD), blk),
                   pl.BlockSpec((None, rows, n_gate), blk)],
        out_shape=[jax.ShapeDtypeStruct((depth, D, 8 * W + F_PAD), BF16),
                   jax.ShapeDtypeStruct((depth, D, n_gate), BF16)],
        compiler_params=_params("parallel", "parallel"),
        name="regroup_w_in",
    )(w_in)


def _block_diag(blocks):
    G, a, b = blocks.shape
    eye = jnp.eye(G, dtype=blocks.dtype)
    return (eye[:, None, :, None] * blocks[:, :, None, :]).reshape(G * a, G * b)


def _ssm_params(lam_re, lam_im, log_dt, b_re, b_im, c_re, c_im):
    dt = jnp.exp(log_dt)[:, None]
    lr = jnp.minimum(lam_re, -1e-4)
    li = lam_im
    mag = jnp.exp(lr * dt)
    lbr = mag * jnp.cos(li * dt)
    lbi = mag * jnp.sin(li * dt)
    den = lr * lr + li * li
    cr = ((lbr - 1.0) * lr + lbi * li) / den
    ci = (lbi * lr - (lbr - 1.0) * li) / den
    bbr = cr[..., None] * b_re - ci[..., None] * b_im
    bbi = cr[..., None] * b_im + ci[..., None] * b_re
    bd = jnp.concatenate([_block_diag(bbr.transpose(0, 2, 1)),
                          _block_diag(bbi.transpose(0, 2, 1))], axis=1)
    cd = jnp.concatenate([_block_diag(c_re.transpose(0, 2, 1)),
                          _block_diag(-c_im.transpose(0, 2, 1))], axis=0)
    return bd.astype(BF16), lbr.reshape(1, -1), lbi.reshape(1, -1), cd.astype(BF16)


def kernel(x, p, norm_g, ffn_w_gate, ffn_w_up, ffn_w_down, w_in, f_bias, pool_w, pool_scale,
           ssm_lam_re, ssm_lam_im, ssm_log_dt, ssm_b_re, ssm_b_im, ssm_c_re, ssm_c_im, ssm_d,
           ssm_w_glu, conv_w, w_branch, w_out, ple_w_gate, ple_w_proj, final_g):
    B, S, D = x.shape
    depth = norm_g.shape[0]
    T = B * S
    W = BRANCH_WIDTH
    assert B == V7X_SUBLANES, "the SSM keeps one batch row per sublane"
    h = x.reshape(T, D)
    wg_all, wu_all, wd_all = (w.astype(BF16) for w in (ffn_w_gate, ffn_w_up, ffn_w_down))
    wpg_all, wpp_all = ple_w_gate.astype(BF16), ple_w_proj.astype(BF16)
    p3 = p.reshape(depth, T, p.shape[-1])
    w_mix_all, w_gate_all = _regroup_w_in(w_in)
    for i in range(depth):
        g = norm_g[i].reshape(-1, 1, D)
        h = _ffn(h, g[0], wg_all, wu_all, wd_all, (i, 0))

        fb = jnp.zeros((1, F_PAD), F32).at[0, 0:ATT_HEADS].set(f_bias[i])
        h3 = h.reshape(B, S, D)
        qt, k, vt, ccol, crow, y_pool, xs, y_conv = _proj(
            h3, g[1], w_mix_all, i, fb, _block_diag(pool_w[i]).astype(BF16),
            pool_scale[i].reshape(1, W), conv_w[i])
        y_att = _attention(qt, k, vt, crow, ccol)
        bd, lr, li, cd = _ssm_params(ssm_lam_re[i], ssm_lam_im[i], ssm_log_dt[i], ssm_b_re[i],
                                     ssm_b_im[i], ssm_c_re[i], ssm_c_im[i])
        y_ssm = _ssm(xs, bd, lr, li, cd, ssm_d[i].reshape(1, W), ssm_w_glu[i].astype(BF16))
        h3 = _merge(h3, g[1], w_gate_all, i, y_att, y_pool, y_ssm, y_conv,
                    w_branch[i].astype(BF16), w_out[i].astype(BF16))
        h = h3.reshape(T, D)

        h = _ffn(h, g[2], wg_all, wu_all, wd_all, (i, 1),
                 ple_args=(p3, g[3], wpg_all, wpp_all),
                 final_g=final_g.reshape(1, D) if i == depth - 1 else None)
    return h.reshape(B, S, D)
```

```python
import functools

import jax
import jax.numpy as jnp
from jax import lax
from jax.experimental import pallas as pl
from jax.experimental.pallas import tpu as pltpu

F32 = jnp.float32
BF16 = jnp.bfloat16

EPS = 1e-6
NEG_INF = -1e30
LOG2E = 1.4426950408889634

ATT_HEADS = 4
ATT_HEAD_DIM = 64
BRANCH_WIDTH = 256
N_BRANCH = 4
POOL_WINDOWS = (2, 4, 8, 16)
POOL_GROUP_DIM = 64
POOL_HALO = 16
SSM_GROUPS = 16
SSM_GROUP_DIM = 16
SSM_STATE = 64
SSM_CHANNELS = SSM_GROUPS * SSM_STATE
CONV_WIDTH = 3
CONV_HALO = 8
F_PAD = 128
C_ROWS = 8
ONES_ROWS = 16

V7X_SUBLANES = 8
V7X_VMEM_LIMIT_BYTES = 56 * 1024 * 1024

FFN_TILE = 512
PROJ_TILE = 512
MERGE_TILE = 512
ATT_TILE = 512
SSM_STEPS = 64
REGROUP_ROWS = 256


def _sigmoid(x):
    return 0.5 * jnp.tanh(0.5 * x) + 0.5


def _rmsnorm(x, g):
    return x * lax.rsqrt(jnp.mean(x * x, axis=-1, keepdims=True) + EPS) * g


def _dot(a, b):
    return jnp.dot(a, b, preferred_element_type=F32)


def _const_spec(shape):
    zeros = (0,) * len(shape)
    return pl.BlockSpec(shape, lambda *_: zeros, pipeline_mode=pl.Buffered(1))


def _params(*semantics):
    return pltpu.CompilerParams(dimension_semantics=semantics,
                                vmem_limit_bytes=V7X_VMEM_LIMIT_BYTES)


def _ffn_kernel(*refs, ple, final):
    h_ref, g_ref, wg_ref, wu_ref, wd_ref = refs[:5]
    rest = refs[5:]
    x = h_ref[...]
    xn = _rmsnorm(x, g_ref[...]).astype(BF16)
    a = _dot(xn, wg_ref[...])
    b = _dot(xn, wu_ref[...])
    act = (a * _sigmoid(a) * b).astype(BF16)
    y = x + 0.5 * _dot(act, wd_ref[...])
    if ple:
        p_ref, gp_ref, wpg_ref, wpp_ref = rest[:4]
        rest = rest[4:]
        yn = _rmsnorm(y, gp_ref[...]).astype(BF16)
        gate = _sigmoid(_dot(yn, wpg_ref[...]))
        y = y + gate * _dot(p_ref[...].astype(BF16), wpp_ref[...])
    if final:
        gf_ref = rest[0]
        rest = rest[1:]
        y = _rmsnorm(y, gf_ref[...])
    (o_ref,) = rest
    o_ref[...] = y


def _picked_spec(shape, index):
    lead = len(index)
    block = (None,) * lead + tuple(shape[lead:])
    full = tuple(index) + (0,) * (len(shape) - lead)
    return pl.BlockSpec(block, lambda *_: full, pipeline_mode=pl.Buffered(1))


def _ffn(h, g, wg, wu, wd, which, ple_args=None, final_g=None):
    T, D = h.shape
    F = wg.shape[-1]
    tm = min(FFN_TILE, T)
    row = lambda i: (i, 0)
    in_specs = [pl.BlockSpec((tm, D), row), _const_spec((1, D)), _picked_spec(wg.shape, which),
                _picked_spec(wu.shape, which), _picked_spec(wd.shape, which)]
    args = [h, g, wg, wu, wd]
    if ple_args is not None:
        p, gp, wpg, wpp = ple_args
        P = p.shape[-1]
        layer = which[0]
        in_specs += [pl.BlockSpec((None, tm, P), lambda i: (layer, i, 0)), _const_spec((1, D)),
                     _picked_spec(wpg.shape, (layer,)), _picked_spec(wpp.shape, (layer,))]
        args += [p, gp, wpg, wpp]
    if final_g is not None:
        in_specs.append(_const_spec((1, D)))
        args.append(final_g)
    return pl.pallas_call(
        functools.partial(_ffn_kernel, ple=ple_args is not None, final=final_g is not None),
        grid=(T // tm,),
        in_specs=in_specs,
        out_specs=pl.BlockSpec((tm, D), row),
        out_shape=jax.ShapeDtypeStruct((T, D), F32),
        compiler_params=_params("parallel"),
        name="ffn",
    )(*args)


def _proj_kernel(h_ref, g_ref, w_ref, fb_ref, pw_ref, ps_ref, cw_ref,
                 qt_ref, k_ref, vt_ref, ccol_ref, crow_ref, pool_ref, xs_ref, conv_ref,
                 carry_ref, xp_ext, zc_ext, *, tm):
    s_idx = pl.program_id(1)
    W = BRANCH_WIDTH

    @pl.when(s_idx == 0)
    def _():
        carry_ref[...] = jnp.zeros_like(carry_ref)
        xp_ext[0:POOL_HALO, :] = jnp.zeros((POOL_HALO, W), F32)
        zc_ext[0:CONV_HALO, :] = jnp.zeros((CONV_HALO, W), F32)

    u = _rmsnorm(h_ref[...], g_ref[...]).astype(BF16)
    z = _dot(u, w_ref[:, 3 * W:8 * W + F_PAD])
    z_att = _dot(u, w_ref[:, 0:3 * W])
    xp = z[:, 0:W]
    xs_ref[...] = z[:, W:2 * W].astype(BF16)
    cb = z[:, 2 * W:3 * W]
    zc = z[:, 3 * W:4 * W] * z[:, 4 * W:5 * W]
    zf = z[:, 5 * W:5 * W + F_PAD]

    t = (zf + fb_ref[...]).T[0:C_ROWS, :]
    lf = jnp.minimum(t, 0.0) - jnp.log1p(jnp.exp(-jnp.abs(t)))
    hi = lf.astype(BF16).astype(F32)
    mid = (lf - hi).astype(BF16).astype(F32)
    lo = lf - hi - mid
    pieces = jnp.concatenate([hi, mid, lo, jnp.zeros_like(lo)], axis=0).astype(BF16)
    rows = lax.broadcasted_iota(jnp.int32, (tm, tm), 0)
    cols = lax.broadcasted_iota(jnp.int32, (tm, tm), 1)
    upper = jnp.where(rows <= cols, 1.0, 0.0).astype(BF16)
    sums = _dot(pieces, upper)
    c = (sums[0:C_ROWS] + sums[C_ROWS:2 * C_ROWS] + sums[2 * C_ROWS:3 * C_ROWS]
         + carry_ref[:, 0:1])
    carry_ref[...] = jnp.broadcast_to(c[:, tm - 1:tm], carry_ref.shape)
    c2 = c * LOG2E
    crow_ref[...] = c2
    ccol_ref[...] = jnp.concatenate(
        [c2, jnp.zeros((F_PAD - C_ROWS, tm), F32)], axis=0).T

    xp_ext[POOL_HALO:POOL_HALO + tm, :] = xp
    half = W // 2

    def shifted(k, lo_lane):
        return xp_ext[POOL_HALO - k:POOL_HALO - k + tm, lo_lane:lo_lane + half]

    def window(lo_lane, n):
        acc = shifted(0, lo_lane)
        for k in range(1, n):
            acc = acc + shifted(k, lo_lane)
        return acc

    lane = lax.broadcasted_iota(jnp.int32, (tm, half), 1)
    pos = s_idx * tm + lax.broadcasted_iota(jnp.int32, (tm, half), 0)
    first = lane < POOL_GROUP_DIM
    halves = []
    for hidx in range(2):
        w_small, w_big = POOL_WINDOWS[2 * hidx], POOL_WINDOWS[2 * hidx + 1]
        lo_lane = hidx * half
        sum_small = window(lo_lane, w_small)
        sum_big = sum_small
        for k in range(w_small, w_big):
            sum_big = sum_big + shifted(k, lo_lane)
        wsum = jnp.where(first, sum_small, sum_big)
        win = jnp.where(first, w_small, w_big)
        count = jnp.minimum(pos + 1, win).astype(F32)
        halves.append(wsum / count - shifted(0, lo_lane))
    pooled = jnp.concatenate(halves, axis=1).astype(BF16)
    pool_ref[...] = (_dot(pooled, pw_ref[...]) * ps_ref[...]).astype(BF16)
    xp_ext[0:POOL_HALO, :] = xp_ext[tm:tm + POOL_HALO, :]

    zc_ext[CONV_HALO:CONV_HALO + tm, :] = zc
    y = cw_ref[CONV_WIDTH - 1:CONV_WIDTH, :] * zc
    for j in range(CONV_WIDTH - 1):
        back = CONV_WIDTH - 1 - j
        y = y + cw_ref[j:j + 1, :] * zc_ext[CONV_HALO - back:CONV_HALO - back + tm, :]
    conv_ref[...] = (cb * y).astype(BF16)
    zc_ext[0:CONV_HALO, :] = zc_ext[tm:tm + CONV_HALO, :]

    qt_ref[...] = (z_att[:, 0:W] * (LOG2E * ATT_HEAD_DIM ** -0.5)).T.astype(BF16)
    k_ref[...] = z_att[:, W:2 * W].astype(BF16)
    vt_ref[...] = z_att[:, 2 * W:3 * W].T.astype(BF16)


def _proj(h3, g, w, layer, fb, pw, ps, cw):
    B, S, D = h3.shape
    W = BRANCH_WIDTH
    tm = min(PROJ_TILE, S)
    tok = lambda b, s: (b, s, 0)
    act = jax.ShapeDtypeStruct((B, S, W), BF16)
    act_spec = pl.BlockSpec((None, tm, W), tok)
    act_t = jax.ShapeDtypeStruct((B, W, S), BF16)
    act_t_spec = pl.BlockSpec((None, W, tm), lambda b, s: (b, 0, s))
    return pl.pallas_call(
        functools.partial(_proj_kernel, tm=tm),
        grid=(B, S // tm),
        in_specs=[pl.BlockSpec((None, tm, D), tok), _const_spec((1, D)),
                  _picked_spec(w.shape, (layer,)), _const_spec((1, F_PAD)), _const_spec((W, W)),
                  _const_spec((1, W)), _const_spec((CONV_WIDTH, W))],
        out_specs=[act_t_spec, act_spec, act_t_spec,
                   pl.BlockSpec((None, tm, F_PAD), tok),
                   pl.BlockSpec((None, C_ROWS, tm), lambda b, s: (b, 0, s)),
                   act_spec, act_spec, act_spec],
        out_shape=[act_t, act, act_t,
                   jax.ShapeDtypeStruct((B, S, F_PAD), F32),
                   jax.ShapeDtypeStruct((B, C_ROWS, S), F32),
                   act, act, act],
        scratch_shapes=[pltpu.VMEM((C_ROWS, F_PAD), F32),
                        pltpu.VMEM((POOL_HALO + tm, W), F32),
                        pltpu.VMEM((CONV_HALO + tm, W), F32)],
        compiler_params=_params("arbitrary", "arbitrary"),
        name="proj",
    )(h3, g, w, fb, pw, ps, cw)


def _att_kernel(qt_ref, k_ref, vt_ref, crow_ref, ccol_ref, o_ref, *, tile):
    qi = pl.program_id(1)
    W = BRANCH_WIDTH
    Dh = ATT_HEAD_DIM
    qt = qt_ref[...]
    feat = lax.broadcasted_iota(jnp.int32, (W, tile), 0)
    qts = [jnp.where((feat >= h * Dh) & (feat < (h + 1) * Dh), qt, jnp.zeros_like(qt))
           for h in range(ATT_HEADS)]
    c_t = [crow_ref[h:h + 1, :] for h in range(ATT_HEADS)]
    kv_pos = lax.broadcasted_iota(jnp.int32, (tile, tile), 0)
    q_pos = lax.broadcasted_iota(jnp.int32, (tile, tile), 1)

    def block(j, stats, diagonal):
        start = pl.multiple_of(j * tile, tile)
        kj = k_ref[pl.ds(start, tile), :]
        ts = [_dot(kj, qts[h]) - ccol_ref[pl.ds(start, tile), h:h + 1]
              for h in range(ATT_HEADS)]
        new = []
        for h in range(ATT_HEADS):
            m, acc = stats[h]
            t = ts[h]
            if diagonal:
                t = jnp.where(kv_pos <= q_pos, t, NEG_INF)
            m_new = jnp.maximum(m, jnp.max(t, axis=0, keepdims=True) + c_t[h])
            alpha = jnp.exp2(m - m_new)
            p = jnp.exp2(t + (c_t[h] - m_new)).astype(BF16)
            vth = jnp.concatenate([vt_ref[h * Dh:(h + 1) * Dh, pl.ds(start, tile)], ones],
                                  axis=0)
            acc = alpha * acc + _dot(vth, p)
            new.append((m_new, acc))
        return tuple(new)

    ones = jnp.ones((ONES_ROWS, tile), BF16)
    init = tuple((jnp.full((1, tile), NEG_INF, F32), jnp.zeros((Dh + ONES_ROWS, tile), F32))
                 for _ in range(ATT_HEADS))
    stats = lax.fori_loop(0, qi, lambda j, c: block(j, c, False), init)
    stats = block(qi, stats, True)
    out_t = jnp.concatenate([acc[0:Dh] / acc[Dh:Dh + 1] for (_, acc) in stats],
                            axis=0)
    o_ref[...] = out_t.T.astype(o_ref.dtype)


def _attention(qt, k, vt, crow, ccol):
    B, S, W = k.shape
    tile = min(ATT_TILE, S)
    seq = lambda b, i: (b, 0, 0)
    qcol = lambda b, i: (b, 0, i)
    return pl.pallas_call(
        functools.partial(_att_kernel, tile=tile),
        grid=(B, S // tile),
        in_specs=[pl.BlockSpec((None, W, tile), qcol),
                  pl.BlockSpec((None, S, W), seq),
                  pl.BlockSpec((None, W, S), seq),
                  pl.BlockSpec((None, C_ROWS, tile), qcol),
                  pl.BlockSpec((None, S, F_PAD), seq)],
        out_specs=pl.BlockSpec((None, tile, W), lambda b, i: (b, i, 0)),
        out_shape=jax.ShapeDtypeStruct((B, S, W), BF16),
        compiler_params=_params("parallel", "arbitrary"),
        name="attention",
    )(qt, k, vt, crow, ccol)


def _dot_row_halves(lhs_ref, rhs, cast=None):
    half = lhs_ref.shape[0] // 2
    parts = []
    for r in range(2):
        lhs = lhs_ref[r * half:(r + 1) * half, :]
        parts.append(_dot(lhs if cast is None else lhs.astype(cast), rhs))
    return jnp.concatenate(parts, axis=0)


def _ssm_kernel(u_ref, perm_ref, perm_t_ref, bd_ref, lr_ref, li_ref, cd_ref, d_ref, wglu_ref,
                o_ref, state_ref, x_ref, *, steps, batch):
    N = SSM_CHANNELS
    W = BRANCH_WIDTH
    rows = steps * batch

    @pl.when(pl.program_id(0) == 0)
    def _():
        state_ref[...] = jnp.zeros_like(state_ref)

    u = _dot_row_halves(perm_ref, u_ref[...].reshape(rows, W)).astype(BF16)
    x_ref[...] = _dot(u, bd_ref[...])
    lr = jnp.broadcast_to(lr_ref[...], (batch, N))
    li = jnp.broadcast_to(li_ref[...], (batch, N))

    def step(t, carry):
        sr, si = carry
        r0 = pl.multiple_of(t * batch, batch)
        nr = lr * sr - li * si + x_ref[pl.ds(r0, batch), 0:N]
        ni = lr * si + li * sr + x_ref[pl.ds(r0, batch), N:2 * N]
        x_ref[pl.ds(r0, batch), 0:N] = nr
        x_ref[pl.ds(r0, batch), N:2 * N] = ni
        return nr, ni

    sr, si = lax.fori_loop(0, steps, step, (state_ref[:, 0:N], state_ref[:, N:2 * N]),
                           unroll=True)
    state_ref[:, 0:N] = sr
    state_ref[:, N:2 * N] = si

    y = _dot_row_halves(x_ref, cd_ref[...], cast=BF16) + d_ref[...] * u.astype(F32)
    yg = _dot(y.astype(BF16), wglu_ref[...])
    out = (yg[:, 0:W] * _sigmoid(yg[:, W:2 * W])).astype(BF16)
    o_ref[...] = _dot_row_halves(perm_t_ref, out).astype(o_ref.dtype).reshape(batch, steps, W)


def _ssm(xs, bd, lr, li, cd, d, wglu):
    B, S, W = xs.shape
    steps = min(SSM_STEPS, S)
    rows = steps * B
    N = SSM_CHANNELS
    src = (jnp.arange(rows) % B) * steps + jnp.arange(rows) // B
    perm = (src[:, None] == jnp.arange(rows)[None, :]).astype(BF16)
    chunk = lambda i: (0, i, 0)
    return pl.pallas_call(
        functools.partial(_ssm_kernel, steps=steps, batch=B),
        grid=(S // steps,),
        in_specs=[pl.BlockSpec((B, steps, W), chunk), _const_spec((rows, rows)),
                  _const_spec((rows, rows)), _const_spec((W, 2 * N)),
                  _const_spec((1, N)), _const_spec((1, N)), _const_spec((2 * N, W)),
                  _const_spec((1, W)), _const_spec((W, 2 * W))],
        out_specs=pl.BlockSpec((B, steps, W), chunk),
        out_shape=jax.ShapeDtypeStruct((B, S, W), BF16),
        scratch_shapes=[pltpu.VMEM((B, 2 * N), F32), pltpu.VMEM((rows, 2 * N), F32)],
        compiler_params=_params("arbitrary"),
        name="ssm",
    )(xs, perm, perm.T, bd, lr, li, cd, d, wglu)


def _merge_kernel(h_ref, g_ref, wgate_ref, att_ref, pool_ref, ssm_ref, conv_ref,
                  wbr_ref, wout_ref, o_ref):
    D = h_ref.shape[-1]
    x = h_ref[...]
    u = _rmsnorm(x, g_ref[...]).astype(BF16)
    merged = None
    for n, y_ref in enumerate((att_ref, pool_ref, ssm_ref, conv_ref)):
        gate = _sigmoid(_dot(u, wgate_ref[:, n * D:(n + 1) * D]))
        term = gate * _dot(y_ref[...], wbr_ref[n])
        merged = term if merged is None else merged + term
    o_ref[...] = x + _dot(merged.astype(BF16), wout_ref[...])


def _merge(h3, g, wgate, layer, y_att, y_pool, y_ssm, y_conv, wbr, wout):
    B, S, D = h3.shape
    W = BRANCH_WIDTH
    tm = min(MERGE_TILE, S)
    tok = lambda b, s: (b, s, 0)
    act_spec = pl.BlockSpec((None, tm, W), tok)
    return pl.pallas_call(
        _merge_kernel,
        grid=(B, S // tm),
        in_specs=[pl.BlockSpec((None, tm, D), tok), _const_spec((1, D)),
                  _picked_spec(wgate.shape, (layer,)), act_spec, act_spec, act_spec, act_spec,
                  _const_spec((N_BRANCH, W, D)), _const_spec((D, D))],
        out_specs=pl.BlockSpec((None, tm, D), tok),
        out_shape=jax.ShapeDtypeStruct((B, S, D), F32),
        compiler_params=_params("parallel", "parallel"),
        name="merge",
    )(h3, g, wgate, y_att, y_pool, y_ssm, y_conv, wbr, wout)


def _regroup_kernel(w_ref, mix_ref, gate_ref):
    W = BRANCH_WIDTH
    n_mix = 3 * W + ATT_HEADS
    mix_cols = 8 * W + ATT_HEADS
    w = w_ref[...]
    rows = w.shape[0]
    mix_ref[:, 0:3 * W] = w[:, 0:3 * W].astype(BF16)
    mix_ref[:, 3 * W:8 * W] = w[:, n_mix:mix_cols].astype(BF16)
    mix_ref[:, 8 * W:8 * W + F_PAD] = jnp.concatenate(
        [w[:, 3 * W:n_mix], jnp.zeros((rows, F_PAD - ATT_HEADS), F32)], axis=1).astype(BF16)
    gate_ref[...] = w[:, mix_cols:].astype(BF16)


def _regroup_w_in(w_in):
    depth, D, cols = w_in.shape
    W = BRANCH_WIDTH
    n_gate = cols - (8 * W + ATT_HEADS)
    rows = min(REGROUP_ROWS, D)
    blk = lambda i, r: (i, r, 0)
    return pl.pallas_call(
        _regroup_kernel,
        grid=(depth, D // rows),
        in_specs=[pl.BlockSpec((None, rows, cols), blk)],
        out_specs=[pl.BlockSpec((None, rows, 8 * W + F_PAD), blk),
                   pl.BlockSpec((None, rows, n_gate), blk)],
        out_shape=[jax.ShapeDtypeStruct((depth, D, 8 * W + F_PAD), BF16),
                   jax.ShapeDtypeStruct((depth, D, n_gate), BF16)],
        compiler_params=_params("parallel", "parallel"),
        name="regroup_w_in",
    )(w_in)


def _block_diag(blocks):
    G, a, b = blocks.shape
    eye = jnp.eye(G, dtype=blocks.dtype)
    return (eye[:, None, :, None] * blocks[:, :, None, :]).reshape(G * a, G * b)


def _ssm_params(lam_re, lam_im, log_dt, b_re, b_im, c_re, c_im):
    dt = jnp.exp(log_dt)[:, None]
    lr = jnp.minimum(lam_re, -1e-4)
    li = lam_im
    mag = jnp.exp(lr * dt)
    lbr = mag * jnp.cos(li * dt)
    lbi = mag * jnp.sin(li * dt)
    den = lr * lr + li * li
    cr = ((lbr - 1.0) * lr + lbi * li) / den
    ci = (lbi * lr - (lbr - 1.0) * li) / den
    bbr = cr[..., None] * b_re - ci[..., None] * b_im
    bbi = cr[..., None] * b_im + ci[..., None] * b_re
    bd = jnp.concatenate([_block_diag(bbr.transpose(0, 2, 1)),
                          _block_diag(bbi.transpose(0, 2, 1))], axis=1)
    cd = jnp.concatenate([_block_diag(c_re.transpose(0, 2, 1)),
                          _block_diag(-c_im.transpose(0, 2, 1))], axis=0)
    return bd.astype(BF16), lbr.reshape(1, -1), lbi.reshape(1, -1), cd.astype(BF16)


def kernel(x, p, norm_g, ffn_w_gate, ffn_w_up, ffn_w_down, w_in, f_bias, pool_w, pool_scale,
           ssm_lam_re, ssm_lam_im, ssm_log_dt, ssm_b_re, ssm_b_im, ssm_c_re, ssm_c_im, ssm_d,
           ssm_w_glu, conv_w, w_branch, w_out, ple_w_gate, ple_w_proj, final_g):
    B, S, D = x.shape
    depth = norm_g.shape[0]
    T = B * S
    W = BRANCH_WIDTH
    assert B == V7X_SUBLANES, "the SSM keeps one batch row per sublane"
    h = x.reshape(T, D)
    wg_all, wu_all, wd_all = (w.astype(BF16) for w in (ffn_w_gate, ffn_w_up, ffn_w_down))
    wpg_all, wpp_all = ple_w_gate.astype(BF16), ple_w_proj.astype(BF16)
    p3 = p.reshape(depth, T, p.shape[-1])
    w_mix_all, w_gate_all = _regroup_w_in(w_in)
    for i in range(depth):
        g = norm_g[i].reshape(-1, 1, D)
        h = _ffn(h, g[0], wg_all, wu_all, wd_all, (i, 0))

        fb = jnp.zeros((1, F_PAD), F32).at[0, 0:ATT_HEADS].set(f_bias[i])
        h3 = h.reshape(B, S, D)
        qt, k, vt, ccol, crow, y_pool, xs, y_conv = _proj(
            h3, g[1], w_mix_all, i, fb, _block_diag(pool_w[i]).astype(BF16),
            pool_scale[i].reshape(1, W), conv_w[i])
        y_att = _attention(qt, k, vt, crow, ccol)
        bd, lr, li, cd = _ssm_params(ssm_lam_re[i], ssm_lam_im[i], ssm_log_dt[i], ssm_b_re[i],
                                     ssm_b_im[i], ssm_c_re[i], ssm_c_im[i])
        y_ssm = _ssm(xs, bd, lr, li, cd, ssm_d[i].reshape(1, W), ssm_w_glu[i].astype(BF16))
        h3 = _merge(h3, g[1], w_gate_all, i, y_att, y_pool, y_ssm, y_conv,
                    w_branch[i].astype(BF16), w_out[i].astype(BF16))
        h = h3.reshape(T, D)

        h = _ffn(h, g[2], wg_all, wu_all, wd_all, (i, 1),
                 ple_args=(p3, g[3], wpg_all, wpp_all),
                 final_g=final_g.reshape(1, D) if i == depth - 1 else None)
    return h.reshape(B, S, D)
```

```python
import functools

import jax
import jax.numpy as jnp
from jax import lax
from jax.experimental import pallas as pl
from jax.experimental.pallas import tpu as pltpu

F32 = jnp.float32
BF16 = jnp.bfloat16

EPS = 1e-6
NEG_INF = -1e30
LOG2E = 1.4426950408889634

ATT_HEADS = 4
ATT_HEAD_DIM = 64
BRANCH_WIDTH = 256
N_BRANCH = 4
POOL_WINDOWS = (2, 4, 8, 16)
POOL_GROUP_DIM = 64
POOL_HALO = 16
SSM_GROUPS = 16
SSM_GROUP_DIM = 16
SSM_STATE = 64
SSM_CHANNELS = SSM_GROUPS * SSM_STATE
CONV_WIDTH = 3
CONV_HALO = 8
F_PAD = 128
C_ROWS = 8
ONES_ROWS = 16

V7X_SUBLANES = 8
V7X_VMEM_LIMIT_BYTES = 56 * 1024 * 1024

FFN_TILE = 512
PROJ_TILE = 1024
MERGE_TILE = 1024
ATT_TILE = 512
SSM_STEPS = 64
REGROUP_ROWS = 256


def _sigmoid(x):
    return 0.5 * jnp.tanh(0.5 * x) + 0.5


def _rmsnorm(x, g):
    return x * lax.rsqrt(jnp.mean(x * x, axis=-1, keepdims=True) + EPS) * g


def _dot(a, b):
    return jnp.dot(a, b, preferred_element_type=F32)


def _const_spec(shape):
    zeros = (0,) * len(shape)
    return pl.BlockSpec(shape, lambda *_: zeros, pipeline_mode=pl.Buffered(1))


def _params(*semantics):
    return pltpu.CompilerParams(dimension_semantics=semantics,
                                vmem_limit_bytes=V7X_VMEM_LIMIT_BYTES)


def _ffn_kernel(*refs, ple, final):
    h_ref, g_ref, wg_ref, wu_ref, wd_ref = refs[:5]
    rest = refs[5:]
    x = h_ref[...]
    xn = _rmsnorm(x, g_ref[...]).astype(BF16)
    a = _dot(xn, wg_ref[...])
    b = _dot(xn, wu_ref[...])
    act = (a * _sigmoid(a) * b).astype(BF16)
    y = x + 0.5 * _dot(act, wd_ref[...])
    if ple:
        p_ref, gp_ref, wpg_ref, wpp_ref = rest[:4]
        rest = rest[4:]
        yn = _rmsnorm(y, gp_ref[...]).astype(BF16)
        gate = _sigmoid(_dot(yn, wpg_ref[...]))
        y = y + gate * _dot(p_ref[...].astype(BF16), wpp_ref[...])
    if final:
        gf_ref = rest[0]
        rest = rest[1:]
        y = _rmsnorm(y, gf_ref[...])
    (o_ref,) = rest
    o_ref[...] = y


def _picked_spec(shape, index):
    lead = len(index)
    block = (None,) * lead + tuple(shape[lead:])
    full = tuple(index) + (0,) * (len(shape) - lead)
    return pl.BlockSpec(block, lambda *_: full, pipeline_mode=pl.Buffered(1))


def _ffn(h, g, wg, wu, wd, which, ple_args=None, final_g=None):
    T, D = h.shape
    F = wg.shape[-1]
    tm = min(FFN_TILE, T)
    row = lambda i: (i, 0)
    in_specs = [pl.BlockSpec((tm, D), row), _const_spec((1, D)), _picked_spec(wg.shape, which),
                _picked_spec(wu.shape, which), _picked_spec(wd.shape, which)]
    args = [h, g, wg, wu, wd]
    if ple_args is not None:
        p, gp, wpg, wpp = ple_args
        P = p.shape[-1]
        layer = which[0]
        in_specs += [pl.BlockSpec((None, tm, P), lambda i: (layer, i, 0)), _const_spec((1, D)),
                     _picked_spec(wpg.shape, (layer,)), _picked_spec(wpp.shape, (layer,))]
        args += [p, gp, wpg, wpp]
    if final_g is not None:
        in_specs.append(_const_spec((1, D)))
        args.append(final_g)
    return pl.pallas_call(
        functools.partial(_ffn_kernel, ple=ple_args is not None, final=final_g is not None),
        grid=(T // tm,),
        in_specs=in_specs,
        out_specs=pl.BlockSpec((tm, D), row),
        out_shape=jax.ShapeDtypeStruct((T, D), F32),
        compiler_params=_params("parallel"),
        name="ffn",
    )(*args)


def _proj_kernel(h_ref, g_ref, w_ref, fb_ref, pw_ref, ps_ref, cw_ref,
                 qt_ref, k_ref, vt_ref, ccol_ref, crow_ref, pool_ref, xs_ref, conv_ref,
                 carry_ref, xp_ext, zc_ext, *, tm):
    s_idx = pl.program_id(1)
    W = BRANCH_WIDTH

    @pl.when(s_idx == 0)
    def _():
        carry_ref[...] = jnp.zeros_like(carry_ref)
        xp_ext[0:POOL_HALO, :] = jnp.zeros((POOL_HALO, W), F32)
        zc_ext[0:CONV_HALO, :] = jnp.zeros((CONV_HALO, W), F32)

    u = _rmsnorm(h_ref[...], g_ref[...]).astype(BF16)
    z = _dot(u, w_ref[:, 3 * W:8 * W + F_PAD])
    z_att = _dot(u, w_ref[:, 0:3 * W])
    xp = z[:, 0:W]
    xs_ref[...] = z[:, W:2 * W].astype(BF16)
    cb = z[:, 2 * W:3 * W]
    zc = z[:, 3 * W:4 * W] * z[:, 4 * W:5 * W]
    zf = z[:, 5 * W:5 * W + F_PAD]

    t = (zf + fb_ref[...]).T[0:C_ROWS, :]
    lf = jnp.minimum(t, 0.0) - jnp.log1p(jnp.exp(-jnp.abs(t)))
    hi = lf.astype(BF16).astype(F32)
    mid = (lf - hi).astype(BF16).astype(F32)
    lo = lf - hi - mid
    pieces = jnp.concatenate([hi, mid, lo, jnp.zeros_like(lo)], axis=0).astype(BF16)
    rows = lax.broadcasted_iota(jnp.int32, (tm, tm), 0)
    cols = lax.broadcasted_iota(jnp.int32, (tm, tm), 1)
    upper = jnp.where(rows <= cols, 1.0, 0.0).astype(BF16)
    sums = _dot(pieces, upper)
    c = (sums[0:C_ROWS] + sums[C_ROWS:2 * C_ROWS] + sums[2 * C_ROWS:3 * C_ROWS]
         + carry_ref[:, 0:1])
    carry_ref[...] = jnp.broadcast_to(c[:, tm - 1:tm], carry_ref.shape)
    c2 = c * LOG2E
    crow_ref[...] = c2
    ccol_ref[...] = jnp.concatenate(
        [c2, jnp.zeros((F_PAD - C_ROWS, tm), F32)], axis=0).T

    xp_ext[POOL_HALO:POOL_HALO + tm, :] = xp
    half = W // 2

    def shifted(k, lo_lane):
        return xp_ext[POOL_HALO - k:POOL_HALO - k + tm, lo_lane:lo_lane + half]

    def window(lo_lane, n):
        acc = shifted(0, lo_lane)
        for k in range(1, n):
            acc = acc + shifted(k, lo_lane)
        return acc

    lane = lax.broadcasted_iota(jnp.int32, (tm, half), 1)
    pos = s_idx * tm + lax.broadcasted_iota(jnp.int32, (tm, half), 0)
    first = lane < POOL_GROUP_DIM
    halves = []
    for hidx in range(2):
        w_small, w_big = POOL_WINDOWS[2 * hidx], POOL_WINDOWS[2 * hidx + 1]
        lo_lane = hidx * half
        sum_small = window(lo_lane, w_small)
        sum_big = sum_small
        for k in range(w_small, w_big):
            sum_big = sum_big + shifted(k, lo_lane)
        wsum = jnp.where(first, sum_small, sum_big)
        win = jnp.where(first, w_small, w_big)
        count = jnp.minimum(pos + 1, win).astype(F32)
        halves.append(wsum / count - shifted(0, lo_lane))
    pooled = jnp.concatenate(halves, axis=1).astype(BF16)
    pool_ref[...] = (_dot(pooled, pw_ref[...]) * ps_ref[...]).astype(BF16)
    xp_ext[0:POOL_HALO, :] = xp_ext[tm:tm + POOL_HALO, :]

    zc_ext[CONV_HALO:CONV_HALO + tm, :] = zc
    y = cw_ref[CONV_WIDTH - 1:CONV_WIDTH, :] * zc
    for j in range(CONV_WIDTH - 1):
        back = CONV_WIDTH - 1 - j
        y = y + cw_ref[j:j + 1, :] * zc_ext[CONV_HALO - back:CONV_HALO - back + tm, :]
    conv_ref[...] = (cb * y).astype(BF16)
    zc_ext[0:CONV_HALO, :] = zc_ext[tm:tm + CONV_HALO, :]

    qt_ref[...] = (z_att[:, 0:W] * (LOG2E * ATT_HEAD_DIM ** -0.5)).T.astype(BF16)
    k_ref[...] = z_att[:, W:2 * W].astype(BF16)
    vt_ref[...] = z_att[:, 2 * W:3 * W].T.astype(BF16)


def _proj(h3, g, w, layer, fb, pw, ps, cw):
    B, S, D = h3.shape
    W = BRANCH_WIDTH
    tm = min(PROJ_TILE, S)
    tok = lambda b, s: (b, s, 0)
    act = jax.ShapeDtypeStruct((B, S, W), BF16)
    act_spec = pl.BlockSpec((None, tm, W), tok)
    act_t = jax.ShapeDtypeStruct((B, W, S), BF16)
    act_t_spec = pl.BlockSpec((None, W, tm), lambda b, s: (b, 0, s))
    return pl.pallas_call(
        functools.partial(_proj_kernel, tm=tm),
        grid=(B, S // tm),
        in_specs=[pl.BlockSpec((None, tm, D), tok), _const_spec((1, D)),
                  _picked_spec(w.shape, (layer,)), _const_spec((1, F_PAD)), _const_spec((W, W)),
                  _const_spec((1, W)), _const_spec((CONV_WIDTH, W))],
        out_specs=[act_t_spec, act_spec, act_t_spec,
                   pl.BlockSpec((None, tm, F_PAD), tok),
                   pl.BlockSpec((None, C_ROWS, tm), lambda b, s: (b, 0, s)),
                   act_spec, act_spec, act_spec],
        out_shape=[act_t, act, act_t,
                   jax.ShapeDtypeStruct((B, S, F_PAD), F32),
                   jax.ShapeDtypeStruct((B, C_ROWS, S), F32),
                   act, act, act],
        scratch_shapes=[pltpu.VMEM((C_ROWS, F_PAD), F32),
                        pltpu.VMEM((POOL_HALO + tm, W), F32),
                        pltpu.VMEM((CONV_HALO + tm, W), F32)],
        compiler_params=_params("arbitrary", "arbitrary"),
        name="proj",
    )(h3, g, w, fb, pw, ps, cw)


def _att_kernel(qt_ref, k_ref, vt_ref, crow_ref, ccol_ref, o_ref, *, tile):
    qi = pl.program_id(1)
    W = BRANCH_WIDTH
    Dh = ATT_HEAD_DIM
    qt = qt_ref[...]
    feat = lax.broadcasted_iota(jnp.int32, (W, tile), 0)
    qts = [jnp.where((feat >= h * Dh) & (feat < (h + 1) * Dh), qt, jnp.zeros_like(qt))
           for h in range(ATT_HEADS)]
    c_t = [crow_ref[h:h + 1, :] for h in range(ATT_HEADS)]
    kv_pos = lax.broadcasted_iota(jnp.int32, (tile, tile), 0)
    q_pos = lax.broadcasted_iota(jnp.int32, (tile, tile), 1)

    def block(j, stats, diagonal):
        start = pl.multiple_of(j * tile, tile)
        kj = k_ref[pl.ds(start, tile), :]
        ts = [_dot(kj, qts[h]) - ccol_ref[pl.ds(start, tile), h:h + 1]
              for h in range(ATT_HEADS)]
        new = []
        for h in range(ATT_HEADS):
            m, acc = stats[h]
            t = ts[h]
            if diagonal:
                t = jnp.where(kv_pos <= q_pos, t, NEG_INF)
            m_new = jnp.maximum(m, jnp.max(t, axis=0, keepdims=True) + c_t[h])
            alpha = jnp.exp2(m - m_new)
            p = jnp.exp2(t + (c_t[h] - m_new)).astype(BF16)
            vth = jnp.concatenate([vt_ref[h * Dh:(h + 1) * Dh, pl.ds(start, tile)], ones],
                                  axis=0)
            acc = alpha * acc + _dot(vth, p)
            new.append((m_new, acc))
        return tuple(new)

    ones = jnp.ones((ONES_ROWS, tile), BF16)
    init = tuple((jnp.full((1, tile), NEG_INF, F32), jnp.zeros((Dh + ONES_ROWS, tile), F32))
                 for _ in range(ATT_HEADS))
    stats = lax.fori_loop(0, qi, lambda j, c: block(j, c, False), init)
    stats = block(qi, stats, True)
    out_t = jnp.concatenate([acc[0:Dh] / acc[Dh:Dh + 1] for (_, acc) in stats],
                            axis=0)
    o_ref[...] = out_t.T.astype(o_ref.dtype)


def _attention(qt, k, vt, crow, ccol):
    B, S, W = k.shape
    tile = min(ATT_TILE, S)
    seq = lambda b, i: (b, 0, 0)
    qcol = lambda b, i: (b, 0, i)
    return pl.pallas_call(
        functools.partial(_att_kernel, tile=tile),
        grid=(B, S // tile),
        in_specs=[pl.BlockSpec((None, W, tile), qcol),
                  pl.BlockSpec((None, S, W), seq),
                  pl.BlockSpec((None, W, S), seq),
                  pl.BlockSpec((None, C_ROWS, tile), qcol),
                  pl.BlockSpec((None, S, F_PAD), seq)],
        out_specs=pl.BlockSpec((None, tile, W), lambda b, i: (b, i, 0)),
        out_shape=jax.ShapeDtypeStruct((B, S, W), BF16),
        compiler_params=_params("parallel", "arbitrary"),
        name="attention",
    )(qt, k, vt, crow, ccol)


def _dot_row_halves(lhs_ref, rhs, cast=None):
    half = lhs_ref.shape[0] // 2
    parts = []
    for r in range(2):
        lhs = lhs_ref[r * half:(r + 1) * half, :]
        parts.append(_dot(lhs if cast is None else lhs.astype(cast), rhs))
    return jnp.concatenate(parts, axis=0)


def _ssm_kernel(u_ref, perm_ref, perm_t_ref, bd_ref, lr_ref, li_ref, cd_ref, d_ref, wglu_ref,
                o_ref, state_ref, x_ref, *, steps, batch):
    N = SSM_CHANNELS
    W = BRANCH_WIDTH
    rows = steps * batch

    @pl.when(pl.program_id(0) == 0)
    def _():
        state_ref[...] = jnp.zeros_like(state_ref)

    u = _dot_row_halves(perm_ref, u_ref[...].reshape(rows, W)).astype(BF16)
    x_ref[...] = _dot(u, bd_ref[...])
    lr = jnp.broadcast_to(lr_ref[...], (batch, N))
    li = jnp.broadcast_to(li_ref[...], (batch, N))

    def step(t, carry):
        sr, si = carry
        r0 = pl.multiple_of(t * batch, batch)
        nr = lr * sr - li * si + x_ref[pl.ds(r0, batch), 0:N]
        ni = lr * si + li * sr + x_ref[pl.ds(r0, batch), N:2 * N]
        x_ref[pl.ds(r0, batch), 0:N] = nr
        x_ref[pl.ds(r0, batch), N:2 * N] = ni
        return nr, ni

    sr, si = lax.fori_loop(0, steps, step, (state_ref[:, 0:N], state_ref[:, N:2 * N]),
                           unroll=True)
    state_ref[:, 0:N] = sr
    state_ref[:, N:2 * N] = si

    y = _dot_row_halves(x_ref, cd_ref[...], cast=BF16) + d_ref[...] * u.astype(F32)
    yg = _dot(y.astype(BF16), wglu_ref[...])
    out = (yg[:, 0:W] * _sigmoid(yg[:, W:2 * W])).astype(BF16)
    o_ref[...] = _dot_row_halves(perm_t_ref, out).astype(o_ref.dtype).reshape(batch, steps, W)


def _ssm(xs, bd, lr, li, cd, d, wglu):
    B, S, W = xs.shape
    steps = min(SSM_STEPS, S)
    rows = steps * B
    N = SSM_CHANNELS
    src = (jnp.arange(rows) % B) * steps + jnp.arange(rows) // B
    perm = (src[:, None] == jnp.arange(rows)[None, :]).astype(BF16)
    chunk = lambda i: (0, i, 0)
    return pl.pallas_call(
        functools.partial(_ssm_kernel, steps=steps, batch=B),
        grid=(S // steps,),
        in_specs=[pl.BlockSpec((B, steps, W), chunk), _const_spec((rows, rows)),
                  _const_spec((rows, rows)), _const_spec((W, 2 * N)),
                  _const_spec((1, N)), _const_spec((1, N)), _const_spec((2 * N, W)),
                  _const_spec((1, W)), _const_spec((W, 2 * W))],
        out_specs=pl.BlockSpec((B, steps, W), chunk),
        out_shape=jax.ShapeDtypeStruct((B, S, W), BF16),
        scratch_shapes=[pltpu.VMEM((B, 2 * N), F32), pltpu.VMEM((rows, 2 * N), F32)],
        compiler_params=_params("arbitrary"),
        name="ssm",
    )(xs, perm, perm.T, bd, lr, li, cd, d, wglu)


def _merge_kernel(h_ref, g_ref, wgate_ref, att_ref, pool_ref, ssm_ref, conv_ref,
                  wbr_ref, wout_ref, o_ref):
    D = h_ref.shape[-1]
    x = h_ref[...]
    u = _rmsnorm(x, g_ref[...]).astype(BF16)
    merged = None
    for n, y_ref in enumerate((att_ref, pool_ref, ssm_ref, conv_ref)):
        gate = _sigmoid(_dot(u, wgate_ref[:, n * D:(n + 1) * D]))
        term = gate * _dot(y_ref[...], wbr_ref[n])
        merged = term if merged is None else merged + term
    o_ref[...] = x + _dot(merged.astype(BF16), wout_ref[...])


def _merge(h3, g, wgate, layer, y_att, y_pool, y_ssm, y_conv, wbr, wout):
    B, S, D = h3.shape
    W = BRANCH_WIDTH
    tm = min(MERGE_TILE, S)
    tok = lambda b, s: (b, s, 0)
    act_spec = pl.BlockSpec((None, tm, W), tok)
    return pl.pallas_call(
        _merge_kernel,
        grid=(B, S // tm),
        in_specs=[pl.BlockSpec((None, tm, D), tok), _const_spec((1, D)),
                  _picked_spec(wgate.shape, (layer,)), act_spec, act_spec, act_spec, act_spec,
                  _const_spec((N_BRANCH, W, D)), _const_spec((D, D))],
        out_specs=pl.BlockSpec((None, tm, D), tok),
        out_shape=jax.ShapeDtypeStruct((B, S, D), F32),
        compiler_params=_params("parallel", "parallel"),
        name="merge",
    )(h3, g, wgate, y_att, y_pool, y_ssm, y_conv, wbr, wout)


def _regroup_kernel(w_ref, mix_ref, gate_ref):
    W = BRANCH_WIDTH
    n_mix = 3 * W + ATT_HEADS
    mix_cols = 8 * W + ATT_HEADS
    w = w_ref[...]
    rows = w.shape[0]
    mix_ref[:, 0:3 * W] = w[:, 0:3 * W].astype(BF16)
    mix_ref[:, 3 * W:8 * W] = w[:, n_mix:mix_cols].astype(BF16)
    mix_ref[:, 8 * W:8 * W + F_PAD] = jnp.concatenate(
        [w[:, 3 * W:n_mix], jnp.zeros((rows, F_PAD - ATT_HEADS), F32)], axis=1).astype(BF16)
    gate_ref[...] = w[:, mix_cols:].astype(BF16)


def _regroup_w_in(w_in):
    depth, D, cols = w_in.shape
    W = BRANCH_WIDTH
    n_gate = cols - (8 * W + ATT_HEADS)
    rows = min(REGROUP_ROWS, D)
    blk = lambda i, r: (i, r, 0)
    return pl.pallas_call(
        _regroup_kernel,
        grid=(depth, D // rows),
        in_specs=[pl.BlockSpec((None, rows, cols), blk)],
        out_specs=[pl.BlockSpec((None, rows, 8 * W + F_PAD), blk),
                   pl.BlockSpec((None, rows, n_gate), blk)],
        out_shape=[jax.ShapeDtypeStruct((depth, D, 8 * W + F_PAD), BF16),
                   jax.ShapeDtypeStruct((depth, D, n_gate), BF16)],
        compiler_params=_params("parallel", "parallel"),
        name="regroup_w_in",
    )(w_in)


def _block_diag(blocks):
    G, a, b = blocks.shape
    eye = jnp.eye(G, dtype=blocks.dtype)
    return (eye[:, None, :, None] * blocks[:, :, None, :]).reshape(G * a, G * b)


def _ssm_params(lam_re, lam_im, log_dt, b_re, b_im, c_re, c_im):
    dt = jnp.exp(log_dt)[:, None]
    lr = jnp.minimum(lam_re, -1e-4)
    li = lam_im
    mag = jnp.exp(lr * dt)
    lbr = mag * jnp.cos(li * dt)
    lbi = mag * jnp.sin(li * dt)
    den = lr * lr + li * li
    cr = ((lbr - 1.0) * lr + lbi * li) / den
    ci = (lbi * lr - (lbr - 1.0) * li) / den
    bbr = cr[..., None] * b_re - ci[..., None] * b_im
    bbi = cr[..., None] * b_im + ci[..., None] * b_re
    bd = jnp.concatenate([_block_diag(bbr.transpose(0, 2, 1)),
                          _block_diag(bbi.transpose(0, 2, 1))], axis=1)
    cd = jnp.concatenate([_block_diag(c_re.transpose(0, 2, 1)),
                          _block_diag(-c_im.transpose(0, 2, 1))], axis=0)
    return bd.astype(BF16), lbr.reshape(1, -1), lbi.reshape(1, -1), cd.astype(BF16)


def kernel(x, p, norm_g, ffn_w_gate, ffn_w_up, ffn_w_down, w_in, f_bias, pool_w, pool_scale,
           ssm_lam_re, ssm_lam_im, ssm_log_dt, ssm_b_re, ssm_b_im, ssm_c_re, ssm_c_im, ssm_d,
           ssm_w_glu, conv_w, w_branch, w_out, ple_w_gate, ple_w_proj, final_g):
    B, S, D = x.shape
    depth = norm_g.shape[0]
    T = B * S
    W = BRANCH_WIDTH
    assert B == V7X_SUBLANES, "the SSM keeps one batch row per sublane"
    h = x.reshape(T, D)
    wg_all, wu_all, wd_all = (w.astype(BF16) for w in (ffn_w_gate, ffn_w_up, ffn_w_down))
    wpg_all, wpp_all = ple_w_gate.astype(BF16), ple_w_proj.astype(BF16)
    p3 = p.reshape(depth, T, p.shape[-1])
    w_mix_all, w_gate_all = _regroup_w_in(w_in)
    for i in range(depth):
        g = norm_g[i].reshape(-1, 1, D)
        h = _ffn(h, g[0], wg_all, wu_all, wd_all, (i, 0))

        fb = jnp.zeros((1, F_PAD), F32).at[0, 0:ATT_HEADS].set(f_bias[i])
        h3 = h.reshape(B, S, D)
        qt, k, vt, ccol, crow, y_pool, xs, y_conv = _proj(
            h3, g[1], w_mix_all, i, fb, _block_diag(pool_w[i]).astype(BF16),
            pool_scale[i].reshape(1, W), conv_w[i])
        y_att = _attention(qt, k, vt, crow, ccol)
        bd, lr, li, cd = _ssm_params(ssm_lam_re[i], ssm_lam_im[i], ssm_log_dt[i], ssm_b_re[i],
                                     ssm_b_im[i], ssm_c_re[i], ssm_c_im[i])
        y_ssm = _ssm(xs, bd, lr, li, cd, ssm_d[i].reshape(1, W), ssm_w_glu[i].astype(BF16))
        h3 = _merge(h3, g[1], w_gate_all, i, y_att, y_pool, y_ssm, y_conv,
                    w_branch[i].astype(BF16), w_out[i].astype(BF16))
        h = h3.reshape(T, D)

        h = _ffn(h, g[2], wg_all, wu_all, wd_all, (i, 1),
                 ple_args=(p3, g[3], wpg_all, wpp_all),
                 final_g=final_g.reshape(1, D) if i == depth - 1 else None)
    return h.reshape(B, S, D)
```

```python
import functools

import jax
import jax.numpy as jnp
from jax import lax
from jax.experimental import pallas as pl
from jax.experimental.pallas import tpu as pltpu

F32 = jnp.float32
BF16 = jnp.bfloat16

EPS = 1e-6
NEG_INF = -1e30
LOG2E = 1.4426950408889634

ATT_HEADS = 4
ATT_HEAD_DIM = 64
BRANCH_WIDTH = 256
N_BRANCH = 4
POOL_WINDOWS = (2, 4, 8, 16)
POOL_GROUP_DIM = 64
POOL_HALO = 16
SSM_GROUPS = 16
SSM_GROUP_DIM = 16
SSM_STATE = 64
SSM_CHANNELS = SSM_GROUPS * SSM_STATE
CONV_WIDTH = 3
CONV_HALO = 8
F_PAD = 128
C_ROWS = 8
ONES_ROWS = 16

V7X_SUBLANES = 8
V7X_VMEM_LIMIT_BYTES = 56 * 1024 * 1024

FFN_TILE = 512
PROJ_TILE = 1024
MERGE_TILE = 1024
ATT_TILE = 512
SSM_STEPS = 64
REGROUP_ROWS = 256


def _sigmoid(x):
    return 0.5 * jnp.tanh(0.5 * x) + 0.5


def _rmsnorm(x, g):
    return x * lax.rsqrt(jnp.mean(x * x, axis=-1, keepdims=True) + EPS) * g


def _dot(a, b):
    return jnp.dot(a, b, preferred_element_type=F32)


def _const_spec(shape):
    zeros = (0,) * len(shape)
    return pl.BlockSpec(shape, lambda *_: zeros, pipeline_mode=pl.Buffered(1))


def _params(*semantics):
    return pltpu.CompilerParams(dimension_semantics=semantics,
                                vmem_limit_bytes=V7X_VMEM_LIMIT_BYTES)


def _ffn_kernel(*refs, ple, final):
    _ffn_math(refs[0][...], refs[1:], ple, final)


def _ffn_math(x, refs, ple, final):
    g_ref, wg_ref, wu_ref, wd_ref = refs[:4]
    rest = refs[4:]
    xn = _rmsnorm(x, g_ref[...]).astype(BF16)
    a = _dot(xn, wg_ref[...])
    b = _dot(xn, wu_ref[...])
    act = (a * _sigmoid(a) * b).astype(BF16)
    y = x + 0.5 * _dot(act, wd_ref[...])
    if ple:
        p_ref, gp_ref, wpg_ref, wpp_ref = rest[:4]
        rest = rest[4:]
        yn = _rmsnorm(y, gp_ref[...]).astype(BF16)
        gate = _sigmoid(_dot(yn, wpg_ref[...]))
        y = y + gate * _dot(p_ref[...].astype(BF16), wpp_ref[...])
    if final:
        gf_ref = rest[0]
        rest = rest[1:]
        y = _rmsnorm(y, gf_ref[...])
    (o_ref,) = rest
    o_ref[...] = y


def _picked_spec(shape, index):
    lead = len(index)
    block = (None,) * lead + tuple(shape[lead:])
    full = tuple(index) + (0,) * (len(shape) - lead)
    return pl.BlockSpec(block, lambda *_: full, pipeline_mode=pl.Buffered(1))


def _ffn(h, g, wg, wu, wd, which, ple_args=None, final_g=None):
    T, D = h.shape
    F = wg.shape[-1]
    tm = min(FFN_TILE, T)
    row = lambda i: (i, 0)
    in_specs = [pl.BlockSpec((tm, D), row), _const_spec((1, D)), _picked_spec(wg.shape, which),
                _picked_spec(wu.shape, which), _picked_spec(wd.shape, which)]
    args = [h, g, wg, wu, wd]
    if ple_args is not None:
        p, gp, wpg, wpp = ple_args
        P = p.shape[-1]
        layer = which[0]
        in_specs += [pl.BlockSpec((None, tm, P), lambda i: (layer, i, 0)), _const_spec((1, D)),
                     _picked_spec(wpg.shape, (layer,)), _picked_spec(wpp.shape, (layer,))]
        args += [p, gp, wpg, wpp]
    if final_g is not None:
        in_specs.append(_const_spec((1, D)))
        args.append(final_g)
    return pl.pallas_call(
        functools.partial(_ffn_kernel, ple=ple_args is not None, final=final_g is not None),
        grid=(T // tm,),
        in_specs=in_specs,
        out_specs=pl.BlockSpec((tm, D), row),
        out_shape=jax.ShapeDtypeStruct((T, D), F32),
        compiler_params=_params("parallel"),
        name="ffn",
    )(*args)


def _proj_kernel(h_ref, g_ref, w_ref, fb_ref, pw_ref, ps_ref, cw_ref,
                 qt_ref, k_ref, vt_ref, ccol_ref, crow_ref, pool_ref, xs_ref, conv_ref,
                 carry_ref, xp_ext, zc_ext, *, tm):
    s_idx = pl.program_id(1)
    W = BRANCH_WIDTH

    @pl.when(s_idx == 0)
    def _():
        carry_ref[...] = jnp.zeros_like(carry_ref)
        xp_ext[0:POOL_HALO, :] = jnp.zeros((POOL_HALO, W), F32)
        zc_ext[0:CONV_HALO, :] = jnp.zeros((CONV_HALO, W), F32)

    u = _rmsnorm(h_ref[...], g_ref[...]).astype(BF16)
    z = _dot(u, w_ref[:, 3 * W:8 * W + F_PAD])
    z_att = _dot(u, w_ref[:, 0:3 * W])
    xp = z[:, 0:W]
    xs_ref[...] = z[:, W:2 * W].astype(BF16)
    cb = z[:, 2 * W:3 * W]
    zc = z[:, 3 * W:4 * W] * z[:, 4 * W:5 * W]
    zf = z[:, 5 * W:5 * W + F_PAD]

    t = (zf + fb_ref[...]).T[0:C_ROWS, :]
    lf = jnp.minimum(t, 0.0) - jnp.log1p(jnp.exp(-jnp.abs(t)))
    hi = lf.astype(BF16).astype(F32)
    mid = (lf - hi).astype(BF16).astype(F32)
    lo = lf - hi - mid
    pieces = jnp.concatenate([hi, mid, lo, jnp.zeros_like(lo)], axis=0).astype(BF16)
    rows = lax.broadcasted_iota(jnp.int32, (tm, tm), 0)
    cols = lax.broadcasted_iota(jnp.int32, (tm, tm), 1)
    upper = jnp.where(rows <= cols, 1.0, 0.0).astype(BF16)
    sums = _dot(pieces, upper)
    c = (sums[0:C_ROWS] + sums[C_ROWS:2 * C_ROWS] + sums[2 * C_ROWS:3 * C_ROWS]
         + carry_ref[:, 0:1])
    carry_ref[...] = jnp.broadcast_to(c[:, tm - 1:tm], carry_ref.shape)
    c2 = c * LOG2E
    crow_ref[...] = c2
    ccol_ref[...] = jnp.concatenate(
        [c2, jnp.zeros((F_PAD - C_ROWS, tm), F32)], axis=0).T

    xp_ext[POOL_HALO:POOL_HALO + tm, :] = xp
    half = W // 2

    def shifted(k, lo_lane):
        return xp_ext[POOL_HALO - k:POOL_HALO - k + tm, lo_lane:lo_lane + half]

    def window(lo_lane, n):
        acc = shifted(0, lo_lane)
        for k in range(1, n):
            acc = acc + shifted(k, lo_lane)
        return acc

    lane = lax.broadcasted_iota(jnp.int32, (tm, half), 1)
    pos = s_idx * tm + lax.broadcasted_iota(jnp.int32, (tm, half), 0)
    first = lane < POOL_GROUP_DIM
    halves = []
    for hidx in range(2):
        w_small, w_big = POOL_WINDOWS[2 * hidx], POOL_WINDOWS[2 * hidx + 1]
        lo_lane = hidx * half
        sum_small = window(lo_lane, w_small)
        sum_big = sum_small
        for k in range(w_small, w_big):
            sum_big = sum_big + shifted(k, lo_lane)
        wsum = jnp.where(first, sum_small, sum_big)
        win = jnp.where(first, w_small, w_big)
        count = jnp.minimum(pos + 1, win).astype(F32)
        halves.append(wsum / count - shifted(0, lo_lane))
    pooled = jnp.concatenate(halves, axis=1).astype(BF16)
    pool_ref[...] = (_dot(pooled, pw_ref[...]) * ps_ref[...]).astype(BF16)
    xp_ext[0:POOL_HALO, :] = xp_ext[tm:tm + POOL_HALO, :]

    zc_ext[CONV_HALO:CONV_HALO + tm, :] = zc
    y = cw_ref[CONV_WIDTH - 1:CONV_WIDTH, :] * zc
    for j in range(CONV_WIDTH - 1):
        back = CONV_WIDTH - 1 - j
        y = y + cw_ref[j:j + 1, :] * zc_ext[CONV_HALO - back:CONV_HALO - back + tm, :]
    conv_ref[...] = (cb * y).astype(BF16)
    zc_ext[0:CONV_HALO, :] = zc_ext[tm:tm + CONV_HALO, :]

    qt_ref[...] = (z_att[:, 0:W] * (LOG2E * ATT_HEAD_DIM ** -0.5)).T.astype(BF16)
    k_ref[...] = z_att[:, W:2 * W].astype(BF16)
    vt_ref[...] = z_att[:, 2 * W:3 * W].T.astype(BF16)


def _proj(h3, g, w, layer, fb, pw, ps, cw):
    B, S, D = h3.shape
    W = BRANCH_WIDTH
    tm = min(PROJ_TILE, S)
    tok = lambda b, s: (b, s, 0)
    act = jax.ShapeDtypeStruct((B, S, W), BF16)
    act_spec = pl.BlockSpec((None, tm, W), tok)
    act_t = jax.ShapeDtypeStruct((B, W, S), BF16)
    act_t_spec = pl.BlockSpec((None, W, tm), lambda b, s: (b, 0, s))
    return pl.pallas_call(
        functools.partial(_proj_kernel, tm=tm),
        grid=(B, S // tm),
        in_specs=[pl.BlockSpec((None, tm, D), tok), _const_spec((1, D)),
                  _picked_spec(w.shape, (layer,)), _const_spec((1, F_PAD)), _const_spec((W, W)),
                  _const_spec((1, W)), _const_spec((CONV_WIDTH, W))],
        out_specs=[act_t_spec, act_spec, act_t_spec,
                   pl.BlockSpec((None, tm, F_PAD), tok),
                   pl.BlockSpec((None, C_ROWS, tm), lambda b, s: (b, 0, s)),
                   act_spec, act_spec, act_spec],
        out_shape=[act_t, act, act_t,
                   jax.ShapeDtypeStruct((B, S, F_PAD), F32),
                   jax.ShapeDtypeStruct((B, C_ROWS, S), F32),
                   act, act, act],
        scratch_shapes=[pltpu.VMEM((C_ROWS, F_PAD), F32),
                        pltpu.VMEM((POOL_HALO + tm, W), F32),
                        pltpu.VMEM((CONV_HALO + tm, W), F32)],
        compiler_params=_params("arbitrary", "arbitrary"),
        name="proj",
    )(h3, g, w, fb, pw, ps, cw)


def _att_kernel(qt_ref, k_ref, vt_ref, crow_ref, ccol_ref, o_ref, *, tile):
    qi = pl.program_id(1)
    W = BRANCH_WIDTH
    Dh = ATT_HEAD_DIM
    qt = qt_ref[...]
    feat = lax.broadcasted_iota(jnp.int32, (W, tile), 0)
    qts = [jnp.where((feat >= h * Dh) & (feat < (h + 1) * Dh), qt, jnp.zeros_like(qt))
           for h in range(ATT_HEADS)]
    c_t = [crow_ref[h:h + 1, :] for h in range(ATT_HEADS)]
    kv_pos = lax.broadcasted_iota(jnp.int32, (tile, tile), 0)
    q_pos = lax.broadcasted_iota(jnp.int32, (tile, tile), 1)

    def block(j, stats, diagonal):
        start = pl.multiple_of(j * tile, tile)
        kj = k_ref[pl.ds(start, tile), :]
        ts = [_dot(kj, qts[h]) - ccol_ref[pl.ds(start, tile), h:h + 1]
              for h in range(ATT_HEADS)]
        new = []
        for h in range(ATT_HEADS):
            m, acc = stats[h]
            t = ts[h]
            if diagonal:
                t = jnp.where(kv_pos <= q_pos, t, NEG_INF)
            m_new = jnp.maximum(m, jnp.max(t, axis=0, keepdims=True) + c_t[h])
            alpha = jnp.exp2(m - m_new)
            p = jnp.exp2(t + (c_t[h] - m_new)).astype(BF16)
            vth = jnp.concatenate([vt_ref[h * Dh:(h + 1) * Dh, pl.ds(start, tile)], ones],
                                  axis=0)
            acc = alpha * acc + _dot(vth, p)
            new.append((m_new, acc))
        return tuple(new)

    ones = jnp.ones((ONES_ROWS, tile), BF16)
    init = tuple((jnp.full((1, tile), NEG_INF, F32), jnp.zeros((Dh + ONES_ROWS, tile), F32))
                 for _ in range(ATT_HEADS))
    stats = lax.fori_loop(0, qi, lambda j, c: block(j, c, False), init)
    stats = block(qi, stats, True)
    out_t = jnp.concatenate([acc[0:Dh] / acc[Dh:Dh + 1] for (_, acc) in stats],
                            axis=0)
    o_ref[...] = out_t.T.astype(o_ref.dtype)


def _attention(qt, k, vt, crow, ccol):
    B, S, W = k.shape
    tile = min(ATT_TILE, S)
    seq = lambda b, i: (b, 0, 0)
    qcol = lambda b, i: (b, 0, i)
    return pl.pallas_call(
        functools.partial(_att_kernel, tile=tile),
        grid=(B, S // tile),
        in_specs=[pl.BlockSpec((None, W, tile), qcol),
                  pl.BlockSpec((None, S, W), seq),
                  pl.BlockSpec((None, W, S), seq),
                  pl.BlockSpec((None, C_ROWS, tile), qcol),
                  pl.BlockSpec((None, S, F_PAD), seq)],
        out_specs=pl.BlockSpec((None, tile, W), lambda b, i: (b, i, 0)),
        out_shape=jax.ShapeDtypeStruct((B, S, W), BF16),
        compiler_params=_params("parallel", "arbitrary"),
        name="attention",
    )(qt, k, vt, crow, ccol)


def _dot_row_halves(lhs_ref, rhs, cast=None):
    half = lhs_ref.shape[0] // 2
    parts = []
    for r in range(2):
        lhs = lhs_ref[r * half:(r + 1) * half, :]
        parts.append(_dot(lhs if cast is None else lhs.astype(cast), rhs))
    return jnp.concatenate(parts, axis=0)


def _ssm_kernel(u_ref, perm_ref, perm_t_ref, bd_ref, lr_ref, li_ref, cd_ref, d_ref, wglu_ref,
                o_ref, state_ref, x_ref, *, steps, batch):
    N = SSM_CHANNELS
    W = BRANCH_WIDTH
    rows = steps * batch

    @pl.when(pl.program_id(0) == 0)
    def _():
        state_ref[...] = jnp.zeros_like(state_ref)

    u = _dot_row_halves(perm_ref, u_ref[...].reshape(rows, W)).astype(BF16)
    x_ref[...] = _dot(u, bd_ref[...])
    lr = jnp.broadcast_to(lr_ref[...], (batch, N))
    li = jnp.broadcast_to(li_ref[...], (batch, N))

    def step(t, carry):
        sr, si = carry
        r0 = pl.multiple_of(t * batch, batch)
        nr = lr * sr - li * si + x_ref[pl.ds(r0, batch), 0:N]
        ni = lr * si + li * sr + x_ref[pl.ds(r0, batch), N:2 * N]
        x_ref[pl.ds(r0, batch), 0:N] = nr
        x_ref[pl.ds(r0, batch), N:2 * N] = ni
        return nr, ni

    sr, si = lax.fori_loop(0, steps, step, (state_ref[:, 0:N], state_ref[:, N:2 * N]),
                           unroll=True)
    state_ref[:, 0:N] = sr
    state_ref[:, N:2 * N] = si

    y = _dot_row_halves(x_ref, cd_ref[...], cast=BF16) + d_ref[...] * u.astype(F32)
    yg = _dot(y.astype(BF16), wglu_ref[...])
    out = (yg[:, 0:W] * _sigmoid(yg[:, W:2 * W])).astype(BF16)
    o_ref[...] = _dot_row_halves(perm_t_ref, out).astype(o_ref.dtype).reshape(batch, steps, W)


def _ssm(xs, bd, lr, li, cd, d, wglu):
    B, S, W = xs.shape
    steps = min(SSM_STEPS, S)
    rows = steps * B
    N = SSM_CHANNELS
    src = (jnp.arange(rows) % B) * steps + jnp.arange(rows) // B
    perm = (src[:, None] == jnp.arange(rows)[None, :]).astype(BF16)
    chunk = lambda i: (0, i, 0)
    return pl.pallas_call(
        functools.partial(_ssm_kernel, steps=steps, batch=B),
        grid=(S // steps,),
        in_specs=[pl.BlockSpec((B, steps, W), chunk), _const_spec((rows, rows)),
                  _const_spec((rows, rows)), _const_spec((W, 2 * N)),
                  _const_spec((1, N)), _const_spec((1, N)), _const_spec((2 * N, W)),
                  _const_spec((1, W)), _const_spec((W, 2 * W))],
        out_specs=pl.BlockSpec((B, steps, W), chunk),
        out_shape=jax.ShapeDtypeStruct((B, S, W), BF16),
        scratch_shapes=[pltpu.VMEM((B, 2 * N), F32), pltpu.VMEM((rows, 2 * N), F32)],
        compiler_params=_params("arbitrary"),
        name="ssm",
    )(xs, perm, perm.T, bd, lr, li, cd, d, wglu)


def _merge_kernel(h_ref, g_ref, wgate_ref, att_ref, pool_ref, ssm_ref, conv_ref,
                  wbr_ref, wout_ref, o_ref):
    D = h_ref.shape[-1]
    x = h_ref[...]
    u = _rmsnorm(x, g_ref[...]).astype(BF16)
    merged = None
    for n, y_ref in enumerate((att_ref, pool_ref, ssm_ref, conv_ref)):
        gate = _sigmoid(_dot(u, wgate_ref[:, n * D:(n + 1) * D]))
        term = gate * _dot(y_ref[...], wbr_ref[n])
        merged = term if merged is None else merged + term
    o_ref[...] = x + _dot(merged.astype(BF16), wout_ref[...])


def _merge_ffn_kernel(h_ref, g_ref, wgate_ref, att_ref, pool_ref, ssm_ref, conv_ref,
                      wbr_ref, wout_ref, *ffn_refs, ple, final):
    D = h_ref.shape[-1]
    x = h_ref[...]
    u = _rmsnorm(x, g_ref[...]).astype(BF16)
    merged = None
    for n, y_ref in enumerate((att_ref, pool_ref, ssm_ref, conv_ref)):
        gate = _sigmoid(_dot(u, wgate_ref[:, n * D:(n + 1) * D]))
        term = gate * _dot(y_ref[...], wbr_ref[n])
        merged = term if merged is None else merged + term
    _ffn_math(x + _dot(merged.astype(BF16), wout_ref[...]), ffn_refs, ple, final)


def _merge_ffn(h3, g, wgate, layer, y_att, y_pool, y_ssm, y_conv, wbr, wout,
               g_ffn, wg, wu, wd, p, gp, wpg, wpp, final_g=None):
    B, S, D = h3.shape
    W = BRANCH_WIDTH
    tm = min(FFN_TILE, S)
    n_s = S // tm
    tok = lambda b, s: (b, s, 0)
    act_spec = pl.BlockSpec((None, tm, W), tok)
    which = (layer, 1)
    in_specs = [pl.BlockSpec((None, tm, D), tok), _const_spec((1, D)),
                _picked_spec(wgate.shape, (layer,)), act_spec, act_spec, act_spec, act_spec,
                _const_spec((N_BRANCH, W, D)), _const_spec((D, D)),
                _const_spec((1, D)), _picked_spec(wg.shape, which),
                _picked_spec(wu.shape, which), _picked_spec(wd.shape, which),
                pl.BlockSpec((None, tm, p.shape[-1]), lambda b, s: (layer, b * n_s + s, 0)),
                _const_spec((1, D)), _picked_spec(wpg.shape, (layer,)),
                _picked_spec(wpp.shape, (layer,))]
    args = [h3, g, wgate, y_att, y_pool, y_ssm, y_conv, wbr, wout, g_ffn, wg, wu, wd,
            p, gp, wpg, wpp]
    if final_g is not None:
        in_specs.append(_const_spec((1, D)))
        args.append(final_g)
    return pl.pallas_call(
        functools.partial(_merge_ffn_kernel, ple=True, final=final_g is not None),
        grid=(B, n_s),
        in_specs=in_specs,
        out_specs=pl.BlockSpec((None, tm, D), tok),
        out_shape=jax.ShapeDtypeStruct((B, S, D), F32),
        compiler_params=_params("parallel", "parallel"),
        name="merge_ffn",
    )(*args)


def _merge(h3, g, wgate, layer, y_att, y_pool, y_ssm, y_conv, wbr, wout):
    B, S, D = h3.shape
    W = BRANCH_WIDTH
    tm = min(MERGE_TILE, S)
    tok = lambda b, s: (b, s, 0)
    act_spec = pl.BlockSpec((None, tm, W), tok)
    return pl.pallas_call(
        _merge_kernel,
        grid=(B, S // tm),
        in_specs=[pl.BlockSpec((None, tm, D), tok), _const_spec((1, D)),
                  _picked_spec(wgate.shape, (layer,)), act_spec, act_spec, act_spec, act_spec,
                  _const_spec((N_BRANCH, W, D)), _const_spec((D, D))],
        out_specs=pl.BlockSpec((None, tm, D), tok),
        out_shape=jax.ShapeDtypeStruct((B, S, D), F32),
        compiler_params=_params("parallel", "parallel"),
        name="merge",
    )(h3, g, wgate, y_att, y_pool, y_ssm, y_conv, wbr, wout)


def _regroup_kernel(w_ref, mix_ref, gate_ref):
    W = BRANCH_WIDTH
    n_mix = 3 * W + ATT_HEADS
    mix_cols = 8 * W + ATT_HEADS
    w = w_ref[...]
    rows = w.shape[0]
    mix_ref[:, 0:3 * W] = w[:, 0:3 * W].astype(BF16)
    mix_ref[:, 3 * W:8 * W] = w[:, n_mix:mix_cols].astype(BF16)
    mix_ref[:, 8 * W:8 * W + F_PAD] = jnp.concatenate(
        [w[:, 3 * W:n_mix], jnp.zeros((rows, F_PAD - ATT_HEADS), F32)], axis=1).astype(BF16)
    gate_ref[...] = w[:, mix_cols:].astype(BF16)


def _regroup_w_in(w_in):
    depth, D, cols = w_in.shape
    W = BRANCH_WIDTH
    n_gate = cols - (8 * W + ATT_HEADS)
    rows = min(REGROUP_ROWS, D)
    blk = lambda i, r: (i, r, 0)
    return pl.pallas_call(
        _regroup_kernel,
        grid=(depth, D // rows),
        in_specs=[pl.BlockSpec((None, rows, cols), blk)],
        out_specs=[pl.BlockSpec((None, rows, 8 * W + F_PAD), blk),
                   pl.BlockSpec((None, rows, n_gate), blk)],
        out_shape=[jax.ShapeDtypeStruct((depth, D, 8 * W + F_PAD), BF16),
                   jax.ShapeDtypeStruct((depth, D, n_gate), BF16)],
        compiler_params=_params("parallel", "parallel"),
        name="regroup_w_in",
    )(w_in)


def _block_diag(blocks):
    G, a, b = blocks.shape
    eye = jnp.eye(G, dtype=blocks.dtype)
    return (eye[:, None, :, None] * blocks[:, :, None, :]).reshape(G * a, G * b)


def _ssm_params(lam_re, lam_im, log_dt, b_re, b_im, c_re, c_im):
    dt = jnp.exp(log_dt)[:, None]
    lr = jnp.minimum(lam_re, -1e-4)
    li = lam_im
    mag = jnp.exp(lr * dt)
    lbr = mag * jnp.cos(li * dt)
    lbi = mag * jnp.sin(li * dt)
    den = lr * lr + li * li
    cr = ((lbr - 1.0) * lr + lbi * li) / den
    ci = (lbi * lr - (lbr - 1.0) * li) / den
    bbr = cr[..., None] * b_re - ci[..., None] * b_im
    bbi = cr[..., None] * b_im + ci[..., None] * b_re
    bd = jnp.concatenate([_block_diag(bbr.transpose(0, 2, 1)),
                          _block_diag(bbi.transpose(0, 2, 1))], axis=1)
    cd = jnp.concatenate([_block_diag(c_re.transpose(0, 2, 1)),
                          _block_diag(-c_im.transpose(0, 2, 1))], axis=0)
    return bd.astype(BF16), lbr.reshape(1, -1), lbi.reshape(1, -1), cd.astype(BF16)


def kernel(x, p, norm_g, ffn_w_gate, ffn_w_up, ffn_w_down, w_in, f_bias, pool_w, pool_scale,
           ssm_lam_re, ssm_lam_im, ssm_log_dt, ssm_b_re, ssm_b_im, ssm_c_re, ssm_c_im, ssm_d,
           ssm_w_glu, conv_w, w_branch, w_out, ple_w_gate, ple_w_proj, final_g):
    B, S, D = x.shape
    depth = norm_g.shape[0]
    T = B * S
    W = BRANCH_WIDTH
    assert B == V7X_SUBLANES, "the SSM keeps one batch row per sublane"
    h = x.reshape(T, D)
    wg_all, wu_all, wd_all = (w.astype(BF16) for w in (ffn_w_gate, ffn_w_up, ffn_w_down))
    wpg_all, wpp_all = ple_w_gate.astype(BF16), ple_w_proj.astype(BF16)
    p3 = p.reshape(depth, T, p.shape[-1])
    w_mix_all, w_gate_all = _regroup_w_in(w_in)
    for i in range(depth):
        g = norm_g[i].reshape(-1, 1, D)
        h = _ffn(h, g[0], wg_all, wu_all, wd_all, (i, 0))

        fb = jnp.zeros((1, F_PAD), F32).at[0, 0:ATT_HEADS].set(f_bias[i])
        h3 = h.reshape(B, S, D)
        qt, k, vt, ccol, crow, y_pool, xs, y_conv = _proj(
            h3, g[1], w_mix_all, i, fb, _block_diag(pool_w[i]).astype(BF16),
            pool_scale[i].reshape(1, W), conv_w[i])
        y_att = _attention(qt, k, vt, crow, ccol)
        bd, lr, li, cd = _ssm_params(ssm_lam_re[i], ssm_lam_im[i], ssm_log_dt[i], ssm_b_re[i],
                                     ssm_b_im[i], ssm_c_re[i], ssm_c_im[i])
        y_ssm = _ssm(xs, bd, lr, li, cd, ssm_d[i].reshape(1, W), ssm_w_glu[i].astype(BF16))
        h3 = _merge_ffn(h3, g[1], w_gate_all, i, y_att, y_pool, y_ssm, y_conv,
                        w_branch[i].astype(BF16), w_out[i].astype(BF16),
                        g[2], wg_all, wu_all, wd_all, p3, g[3], wpg_all, wpp_all,
                        final_g=final_g.reshape(1, D) if i == depth - 1 else None)
        h = h3.reshape(T, D)
    return h.reshape(B, S, D)
```
